```python
import math
import jax
import jax.numpy as jnp
from jax import lax
import numpy as np

D_MODEL = 1024
BATCH = 1
SEQ = 16384
DEPTH = 1

CHUNK = 64
EPS = 1e-6
D_CONV = D_MODEL
CONV_K = 31
SSM_EXPAND = 2
D_INNER = SSM_EXPAND * D_MODEL
SSM_HEADDIM = 64
SSM_HEADS = D_INNER // SSM_HEADDIM
SSM_GROUPS = 4
SSM_HPG = SSM_HEADS // SSM_GROUPS
SSM_STATE = 128
SSM_CONV_K = 4
N_BRANCHES = 2
IN_GLU = 2 * D_CONV
IN_Z = D_INNER
IN_XBC = D_INNER + 2 * SSM_GROUPS * SSM_STATE
IN_DT = SSM_HEADS
IN_GATE = N_BRANCHES * D_MODEL
D_IN_PROJ = IN_GLU + IN_Z + IN_XBC + IN_DT + IN_GATE
MOE_GROUPS = 8
MOE_EXPERTS_PER_GROUP = 8
N_EXPERTS = MOE_GROUPS * MOE_EXPERTS_PER_GROUP
MOE_TOP_K = 2
D_FF_EXPERT = D_MODEL // 2
MOE_BLOCK = 128

kernel_name = 'hybrid_conformer_ssd_hmoe_block'


def _rmsnorm(x, g):
    xf = x.astype(jnp.float32)
    y = xf * lax.rsqrt(jnp.mean(xf * xf, axis=-1, keepdims=True) + EPS)
    return (y * g.astype(jnp.float32)).astype(x.dtype)


def _layernorm(x, g, b):
    xf = x.astype(jnp.float32)
    mu = jnp.mean(xf, axis=-1, keepdims=True)
    var = jnp.mean(jnp.square(xf - mu), axis=-1, keepdims=True)
    y = (xf - mu) * lax.rsqrt(var + EPS)
    return (y * g.astype(jnp.float32) + b.astype(jnp.float32)).astype(x.dtype)


def _causal_dwconv(x, w, b):
    k = w.shape[0]
    xp = jnp.pad(x, ((0, 0), (k - 1, 0), (0, 0)))
    y = lax.conv_general_dilated(xp, w[:, None, :].astype(x.dtype), (1,), 'VALID',
                                 dimension_numbers=('NWC', 'WIO', 'NWC'),
                                 feature_group_count=x.shape[-1])
    return y + b.astype(x.dtype)


def _ssd(xh, dt, a, bm, cm):
    bsz, seq = xh.shape[0], xh.shape[1]
    nc = seq // CHUNK
    x = (xh * dt[..., None]).reshape(bsz, nc, CHUNK, SSM_GROUPS, SSM_HPG, SSM_HEADDIM)
    ad = (dt * a).reshape(bsz, nc, CHUNK, SSM_GROUPS, SSM_HPG)
    bc = bm.reshape(bsz, nc, CHUNK, SSM_GROUPS, SSM_STATE)
    cc = cm.reshape(bsz, nc, CHUNK, SSM_GROUPS, SSM_STATE)
    a_cs = jnp.cumsum(ad, axis=2)
    causal = jnp.tril(jnp.ones((CHUNK, CHUNK), dtype=bool))[:, :, None, None]
    seg = a_cs[:, :, :, None] - a_cs[:, :, None, :]
    lmat = jnp.exp(jnp.where(causal, seg, -jnp.inf))
    cb = jnp.einsum('bclgn,bcsgn->bclsg', cc, bc)
    y_diag = jnp.einsum('bclsgr,bcsgrp->bclgrp', cb[..., None] * lmat, x)
    decay = jnp.exp(a_cs[:, :, -1:] - a_cs)
    states = jnp.einsum('bclgn,bclgrp->bcgrpn', bc, x * decay[..., None])
    chunk_decay = jnp.exp(a_cs[:, :, -1])

    def step(carry, inp):
        st, dec = inp
        return carry * dec[..., None, None] + st, carry

    init = jnp.zeros((bsz, SSM_GROUPS, SSM_HPG, SSM_HEADDIM, SSM_STATE), jnp.float32)
    _, prev = lax.scan(step, init, (jnp.moveaxis(states, 1, 0), jnp.moveaxis(chunk_decay, 1, 0)))
    prev = jnp.moveaxis(prev, 0, 1)
    y_off = jnp.einsum('bclgn,bcgrpn->bclgrp', cc, prev) * jnp.exp(a_cs)[..., None]
    return (y_diag + y_off).reshape(bsz, seq, SSM_HEADS, SSM_HEADDIM)


def _mixer(n, w_in, cv_dw_w, cv_dw_b, cv_ln_g, cv_ln_b, w_cv_out, ssm_conv_w, ssm_conv_b,
           dt_bias, a_log, d_skip, ssm_norm_g, w_ssm_out, w_mix_out):
    bsz, seq, _ = n.shape
    f32 = jnp.float32
    cuts = [IN_GLU, IN_GLU + IN_Z, IN_GLU + IN_Z + IN_XBC, IN_GLU + IN_Z + IN_XBC + IN_DT]
    glu_in, z, xbc, dt_raw, gate_in = jnp.split(n @ w_in, cuts, axis=-1)
    u_a, u_b = jnp.split(glu_in, 2, axis=-1)
    u = u_a * jax.nn.sigmoid(u_b)
    u = jax.nn.silu(_layernorm(_causal_dwconv(u, cv_dw_w, cv_dw_b), cv_ln_g, cv_ln_b))
    y_cv = u @ w_cv_out
    xbc = jax.nn.silu(_causal_dwconv(xbc, ssm_conv_w, ssm_conv_b))
    xs, bm, cm = jnp.split(xbc, [D_INNER, D_INNER + SSM_GROUPS * SSM_STATE], axis=-1)
    dt = jax.nn.softplus(dt_raw.astype(f32) + dt_bias.astype(f32))
    a = -jnp.exp(a_log.astype(f32))
    xh = xs.reshape(bsz, seq, SSM_HEADS, SSM_HEADDIM).astype(f32)
    y = _ssd(xh, dt, a,
             bm.reshape(bsz, seq, SSM_GROUPS, SSM_STATE).astype(f32),
             cm.reshape(bsz, seq, SSM_GROUPS, SSM_STATE).astype(f32))
    y = y + d_skip.astype(f32)[:, None] * xh
    y = y.reshape(bsz, seq, D_INNER).astype(n.dtype) * jax.nn.silu(z)
    gs = D_INNER // SSM_GROUPS
    y = _rmsnorm(y.reshape(bsz, seq, SSM_GROUPS, gs), ssm_norm_g.reshape(SSM_GROUPS, gs))
    y_ssm = y.reshape(bsz, seq, D_INNER) @ w_ssm_out
    g_cv, g_ssm = jnp.split(jax.nn.sigmoid(gate_in), 2, axis=-1)
    return (g_cv * y_cv + g_ssm * y_ssm) @ w_mix_out


def _hier_moe(xt, w_grp, b_grp, w_er, b_er, w1, w3, w2):
    t, d = xt.shape
    f32 = jnp.float32
    gp = jax.nn.softmax((xt @ w_grp + b_grp).astype(f32), axis=-1)
    g_p, g_idx = lax.top_k(gp, 1)
    el = (xt @ w_er + b_er).astype(f32).reshape(t, MOE_GROUPS, MOE_EXPERTS_PER_GROUP)
    idx = jnp.broadcast_to(g_idx[:, :, None], (t, 1, MOE_EXPERTS_PER_GROUP))
    el_sel = jnp.take_along_axis(el, idx, axis=1)[:, 0]
    ep = jax.nn.softmax(el_sel, axis=-1)
    e_p, e_loc = lax.top_k(ep, MOE_TOP_K)
    e_p = e_p / jnp.sum(e_p, axis=-1, keepdims=True)
    weights = g_p * e_p
    e_glob = g_idx * MOE_EXPERTS_PER_GROUP + e_loc
    n_asg = t * MOE_TOP_K
    e_flat = e_glob.reshape(-1)
    tok_flat = jnp.repeat(jnp.arange(t, dtype=jnp.int32), MOE_TOP_K)
    w_flat = weights.reshape(-1)
    order = jnp.argsort(e_flat)
    e_sorted = e_flat[order]
    counts = jnp.bincount(e_flat, length=N_EXPERTS)
    starts = jnp.cumsum(counts) - counts
    pcounts = (counts + MOE_BLOCK - 1) // MOE_BLOCK * MOE_BLOCK
    pends = jnp.cumsum(pcounts)
    pstarts = pends - pcounts
    dest = pstarts[e_sorted] + (jnp.arange(n_asg) - starts[e_sorted])
    n_blocks = -(-n_asg // MOE_BLOCK) + N_EXPERTS
    n_rows = n_blocks * MOE_BLOCK
    buf_tok = jnp.zeros((n_rows,), jnp.int32).at[dest].set(tok_flat[order])
    buf_w = jnp.zeros((n_rows,), xt.dtype).at[dest].set(w_flat[order].astype(xt.dtype))
    blk_e = jnp.minimum(jnp.searchsorted(pends, jnp.arange(n_blocks) * MOE_BLOCK, side='right'),
                        N_EXPERTS - 1)
    xb = xt[buf_tok].reshape(n_blocks, MOE_BLOCK, d)

    def expert_block(args):
        xblk, e = args
        hdn = jax.nn.silu(xblk @ w1[e]) * (xblk @ w3[e])
        return hdn @ w2[e]

    yb = lax.map(expert_block, (xb, blk_e)).reshape(n_rows, d)
    return jnp.zeros_like(xt).at[buf_tok].add(yb * buf_w[:, None])


def setup_inputs(seed: int = 0) -> dict:
    key = jax.random.key(seed)
    ks = jax.random.split(key, 30)
    f = jnp.float32
    L = DEPTH

    def nrm(k, shape, scale):
        return jax.random.normal(k, shape, f) * scale

    dt0 = jnp.exp(jax.random.uniform(ks[14], (L, SSM_HEADS), f, math.log(1e-3), math.log(1e-1)))
    return {
        'x': nrm(ks[0], (BATCH, SEQ, D_MODEL), 1.0),
        'c': nrm(ks[1], (BATCH, D_MODEL), 1.0),
        'w_ada': nrm(ks[2], (L, D_MODEL, 6 * D_MODEL), 0.5 * D_MODEL ** -0.5),
        'b_ada': nrm(ks[3], (L, 6 * D_MODEL), 0.02),
        'norm1_g': 1.0 + nrm(ks[4], (L, D_MODEL), 0.02),
        'w_in': nrm(ks[5], (L, D_MODEL, D_IN_PROJ), D_MODEL ** -0.5),
        'cv_dw_w': nrm(ks[6], (L, CONV_K, D_CONV), CONV_K ** -0.5),
        'cv_dw_b': nrm(ks[7], (L, D_CONV), 0.02),
        'cv_ln_g': 1.0 + nrm(ks[8], (L, D_CONV), 0.02),
        'cv_ln_b': nrm(ks[9], (L, D_CONV), 0.02),
        'w_cv_out': nrm(ks[10], (L, D_CONV, D_MODEL), D_CONV ** -0.5),
        'ssm_conv_w': nrm(ks[11], (L, SSM_CONV_K, IN_XBC), SSM_CONV_K ** -0.5),
        'ssm_conv_b': nrm(ks[12], (L, IN_XBC), 0.02),
        'dt_bias': dt0 + jnp.log(-jnp.expm1(-dt0)),
        'a_log': jnp.log(jax.random.uniform(ks[15], (L, SSM_HEADS), f, 1.0, 16.0)),
        'd_skip': 1.0 + nrm(ks[16], (L, SSM_HEADS), 0.1),
        'ssm_norm_g': 1.0 + nrm(ks[17], (L, D_INNER), 0.02),
        'w_ssm_out': nrm(ks[18], (L, D_INNER, D_MODEL), D_INNER ** -0.5),
        'w_mix_out': nrm(ks[19], (L, D_MODEL, D_MODEL), D_MODEL ** -0.5),
        'norm2_g': 1.0 + nrm(ks[20], (L, D_MODEL), 0.02),
        'w_grp': nrm(ks[21], (L, D_MODEL, MOE_GROUPS), D_MODEL ** -0.5),
        'b_grp': nrm(ks[22], (L, MOE_GROUPS), 0.01),
        'w_er': nrm(ks[23], (L, D_MODEL, N_EXPERTS), D_MODEL ** -0.5),
        'b_er': nrm(ks[24], (L, N_EXPERTS), 0.01),
        'w1': nrm(ks[25], (L, N_EXPERTS, D_MODEL, D_FF_EXPERT), D_MODEL ** -0.5),
        'w3': nrm(ks[26], (L, N_EXPERTS, D_MODEL, D_FF_EXPERT), D_MODEL ** -0.5),
        'w2': nrm(ks[27], (L, N_EXPERTS, D_FF_EXPERT, D_MODEL), D_FF_EXPERT ** -0.5),
        'final_g': 1.0 + nrm(ks[28], (D_MODEL,), 0.02),
    }


def reference(x, c, w_ada, b_ada, norm1_g, w_in, cv_dw_w, cv_dw_b, cv_ln_g, cv_ln_b, w_cv_out,
              ssm_conv_w, ssm_conv_b, dt_bias, a_log, d_skip, ssm_norm_g, w_ssm_out, w_mix_out,
              norm2_g, w_grp, b_grp, w_er, b_er, w1, w3, w2, final_g):
    bsz, seq, d = x.shape
    h = x
    for i in range(DEPTH):
        mod = jax.nn.silu(c) @ w_ada[i] + b_ada[i]
        sh1, sc1, g1, sh2, sc2, g2 = jnp.split(mod[:, None, :], 6, axis=-1)
        n1 = _rmsnorm(h, norm1_g[i]) * (1.0 + sc1) + sh1
        h = h + g1 * _mixer(n1, w_in[i], cv_dw_w[i], cv_dw_b[i], cv_ln_g[i], cv_ln_b[i], w_cv_out[i],
                            ssm_conv_w[i], ssm_conv_b[i], dt_bias[i], a_log[i], d_skip[i],
                            ssm_norm_g[i], w_ssm_out[i], w_mix_out[i])
        n2 = _rmsnorm(h, norm2_g[i]) * (1.0 + sc2) + sh2
        moe = _hier_moe(n2.reshape(bsz * seq, d), w_grp[i], b_grp[i], w_er[i], b_er[i],
                        w1[i], w3[i], w2[i])
        h = h + g2 * moe.reshape(bsz, seq, d)
    return _rmsnorm(h, final_g)
```

```python
import functools
import math

import jax
import jax.numpy as jnp
from jax import lax
from jax.experimental import pallas as pl
from jax.experimental.pallas import tpu as pltpu

F32 = jnp.float32
BF16 = jnp.bfloat16
I32 = jnp.int32

D_MODEL = 1024
CHUNK = 64
EPS = 1e-6
CONV_K = 31
D_INNER = 2048
HEADDIM = 64
HEADS = 32
GROUPS = 4
HPG = 8
STATE = 128
SSM_CONV_K = 4
D_XBC = D_INNER + 2 * GROUPS * STATE
N_GRP = 8
EPG = 8
N_EXP = 64
D_FF = 512
LANES = 128

P_COLS = 9 * D_MODEL

ROW_BLK = 256
VMEM_LIMIT = 56 * 2**20


def _cparams(sem):
    return pltpu.CompilerParams(dimension_semantics=sem, vmem_limit_bytes=VMEM_LIMIT)


def _sigmoid(x):
    return jax.nn.sigmoid(x)


def _silu(x):
    return x * jax.nn.sigmoid(x)


def _split3(x):
    hi = x.astype(BF16)
    r1 = x - hi.astype(F32)
    mid = r1.astype(BF16)
    lo = (r1 - mid.astype(F32)).astype(BF16)
    return hi, mid, lo


def _dot(a, b):
    return jnp.dot(a, b, preferred_element_type=F32)


def _ada_kernel(c_ref, w_ref, b_ref, o_ref):
    c = c_ref[...]
    s = _silu(c)
    o_ref[...] = jnp.sum(s * w_ref[...], axis=0, keepdims=True) + b_ref[...]


def _ada(c_col, w_ada, b_ada):
    d, n = w_ada.shape
    tn = 512
    return pl.pallas_call(
        _ada_kernel,
        grid=(n // tn,),
        in_specs=[pl.BlockSpec((d, 1), lambda j: (0, 0)),
                  pl.BlockSpec((d, tn), lambda j: (0, j)),
                  pl.BlockSpec((1, tn), lambda j: (0, j))],
        out_specs=pl.BlockSpec((1, tn), lambda j: (0, j)),
        out_shape=jax.ShapeDtypeStruct((1, n), F32),
        compiler_params=_cparams(("arbitrary",)),
        name="ada",
    )(c_col, w_ada, b_ada)


def _inproj_kernel(x_ref, g_ref, sc_ref, sh_ref, w_ref, wdt_ref, p_ref, dt_ref, n1_ref):
    j = pl.program_id(1)

    @pl.when(j == 0)
    def _():
        x = x_ref[...]
        ms = jnp.mean(x * x, axis=-1, keepdims=True)
        y = x * lax.rsqrt(ms + EPS) * g_ref[...]
        n1 = (y * (1.0 + sc_ref[...]) + sh_ref[...]).astype(BF16)
        n1_ref[...] = n1
        dt_ref[...] = _dot(n1, wdt_ref[...])

    p_ref[...] = _dot(n1_ref[...], w_ref[...]).astype(p_ref.dtype)


def _inproj(x2, g, sc, sh, w_main, w_dt, p_dtype):
    s, d = x2.shape
    tm, tn = 1024, 1024
    tm = min(tm, s)
    row = lambda i, j: (0, 0)
    return pl.pallas_call(
        _inproj_kernel,
        grid=(s // tm, P_COLS // tn),
        in_specs=[pl.BlockSpec((tm, d), lambda i, j: (i, 0)),
                  pl.BlockSpec((1, d), row), pl.BlockSpec((1, d), row), pl.BlockSpec((1, d), row),
                  pl.BlockSpec((d, tn), lambda i, j: (0, j)),
                  pl.BlockSpec((d, LANES), row)],
        out_specs=[pl.BlockSpec((tm, tn), lambda i, j: (i, j)),
                   pl.BlockSpec((tm, LANES), lambda i, j: (i, 0))],
        out_shape=[jax.ShapeDtypeStruct((s, P_COLS), p_dtype),
                   jax.ShapeDtypeStruct((s, LANES), F32)],
        scratch_shapes=[pltpu.VMEM((tm, d), BF16)],
        compiler_params=_cparams(("arbitrary", "arbitrary")),
        name="inproj",
    )(x2, g, sc, sh, w_main, w_dt)


CV_TB = 256
CV_HALO = 32
CV_RC = 64
CV_LC = 256


def _conv_kernel(a_ref, b_ref, ah_ref, bh_ref, w_ref, cb_ref, lg_ref, lb_ref, wo_ref, o_ref, u_ref):
    i = pl.program_id(0)
    u_ref[CV_HALO:, :] = a_ref[...].astype(F32) * _sigmoid(b_ref[...].astype(F32))
    uh = ah_ref[...].astype(F32) * _sigmoid(bh_ref[...].astype(F32))
    u_ref[0:CV_HALO, :] = jnp.where(i > 0, uh, 0.0)
    off0 = CV_HALO - (CONV_K - 1)
    for r in range(CV_TB // CV_RC):
        cols = []
        for c in range(D_MODEL // CV_LC):
            lanes = slice(c * CV_LC, (c + 1) * CV_LC)
            acc = jnp.zeros((CV_RC, CV_LC), F32)
            for k in range(CONV_K):
                start = r * CV_RC + off0 + k
                acc = acc + w_ref[k:k + 1, lanes] * u_ref[start:start + CV_RC, lanes]
            cols.append(acc)
        v = jnp.concatenate(cols, axis=1) + cb_ref[...]
        mu = jnp.mean(v, axis=-1, keepdims=True)
        vc = v - mu
        var = jnp.mean(vc * vc, axis=-1, keepdims=True)
        y = vc * lax.rsqrt(var + EPS) * lg_ref[...] + lb_ref[...]
        o_ref[r * CV_RC:(r + 1) * CV_RC, :] = _dot(_silu(y).astype(BF16), wo_ref[...])


def _conv_branch(p, cv_w, cv_b, ln_g, ln_b, w_out):
    s = p.shape[0]
    tb = CV_TB
    hb = tb // CV_HALO
    d = D_MODEL
    row = lambda i: (0, 0)
    halo = lambda col: (lambda i: (jnp.maximum(i * hb - 1, 0), col))
    return pl.pallas_call(
        _conv_kernel,
        grid=(s // tb,),
        in_specs=[pl.BlockSpec((tb, d), lambda i: (i, 2)),
                  pl.BlockSpec((tb, d), lambda i: (i, 6)),
                  pl.BlockSpec((CV_HALO, d), halo(2)),
                  pl.BlockSpec((CV_HALO, d), halo(6)),
                  pl.BlockSpec((32, d), row),
                  pl.BlockSpec((1, d), row), pl.BlockSpec((1, d), row), pl.BlockSpec((1, d), row),
                  pl.BlockSpec((d, d), row)],
        out_specs=pl.BlockSpec((tb, d), lambda i: (i, 0)),
        out_shape=jax.ShapeDtypeStruct((s, d), F32),
        scratch_shapes=[pltpu.VMEM((CV_HALO + tb, d), F32)],
        compiler_params=_cparams(("arbitrary",)),
        name="conv",
    )(p, p, p, p, cv_w, cv_b, ln_g, ln_b, w_out)


SSD_TB = 256
SSD_NC = SSD_TB // CHUNK
SSD_HALO = 8


def _ssd_kernel(xbc_ref, xbch_ref, dt_ref, z_ref, cw_ref, cbias_ref, dtb_ref, alog_ref, dskip_ref,
                ng_ref, wo_ref, e_ref, tril_ref, mgt_ref, trilt_ref, o_ref,
                xc_ref, xconv_ref, yn_ref, st_ref):
    i = pl.program_id(0)

    @pl.when(i == 0)
    def _():
        st_ref[...] = jnp.zeros_like(st_ref)

    xc_ref[SSD_HALO:, :] = xbc_ref[...].astype(F32)
    xc_ref[0:SSD_HALO, :] = jnp.where(i > 0, xbch_ref[...].astype(F32), 0.0)
    off0 = SSD_HALO - (SSM_CONV_K - 1)
    for r in range(SSD_NC):
        acc = jnp.zeros((CHUNK, D_XBC), F32) + cbias_ref[...]
        for k in range(SSM_CONV_K):
            start = r * CHUNK + off0 + k
            acc = acc + cw_ref[k:k + 1, :] * xc_ref[start:start + CHUNK, :]
        xconv_ref[r * CHUNK:(r + 1) * CHUNK, :] = _silu(acc)

    a_row = -jnp.exp(alog_ref[...])
    e_mat = e_ref[...]
    tril = tril_ref[...]
    rr = lax.broadcasted_iota(I32, (2 * CHUNK, LANES), 0)
    cc = lax.broadcasted_iota(I32, (2 * CHUNK, LANES), 1)
    pair_mask = (rr < CHUNK) == (cc < HEADDIM)

    def chunk_body(c, carry):
        r0 = pl.multiple_of(c * CHUNK, CHUNK)
        xs = xconv_ref[pl.ds(r0, CHUNK), 0:D_INNER]
        bm = xconv_ref[pl.ds(r0, CHUNK), D_INNER:D_INNER + GROUPS * STATE].astype(BF16)
        cm = xconv_ref[pl.ds(r0, CHUNK), D_INNER + GROUPS * STATE:D_XBC].astype(BF16)
        dtr = dt_ref[pl.ds(r0, CHUNK), :] + dtb_ref[...]
        dt = jnp.maximum(dtr, 0.0) + jnp.log(1.0 + jnp.exp(-jnp.abs(dtr)))
        ad = dt * a_row
        st = jnp.concatenate([ad, dt], axis=0)
        h3 = _split3(st)
        ex = _dot(h3[0], e_mat) + _dot(h3[1], e_mat) + _dot(h3[2], e_mat)
        adb = ex[0:CHUNK]
        dtb = ex[CHUNK:2 * CHUNK]
        w = jnp.concatenate([adb, adb * mgt_ref[...]], axis=1)
        w3 = _split3(w)
        cs = _dot(tril, w3[0]) + _dot(tril, w3[1]) + _dot(tril, w3[2])
        acs = cs[:, 0:D_INNER]
        seg = cs[:, D_INNER:2 * D_INNER]
        tot = acs[CHUNK - 1:CHUNK, :]
        e_acs = jnp.exp(acs)
        decay = jnp.exp(tot - acs)
        cdec = jnp.exp(tot)
        lmat = jnp.where(trilt_ref[...] > 0.0, jnp.exp(seg), 0.0)
        xdt = xs * dtb
        xd = (xdt * decay).astype(BF16)
        xdt_b = xdt.astype(BF16)
        y_parts = []
        for g in range(GROUPS):
            cg = cm[:, g * STATE:(g + 1) * STATE]
            bg = bm[:, g * STATE:(g + 1) * STATE]
            b2 = jnp.concatenate([bg, bg], axis=0)
            cb2 = lax.dot_general(cg, b2, (((1,), (1,)), ((), ())), preferred_element_type=F32)
            gl = slice(g * HPG * HEADDIM, (g + 1) * HPG * HEADDIM)
            st_g = st_ref[g]
            y_off = _dot(cg, st_g.astype(BF16)) * e_acs[:, gl]
            yd = []
            for k in range(HPG // 2):
                pl_ = slice(g * HPG * HEADDIM + k * LANES, g * HPG * HEADDIM + (k + 1) * LANES)
                gp = (cb2 * lmat[:, pl_]).astype(BF16)
                xp = xdt_b[:, pl_]
                x2 = jnp.where(pair_mask, jnp.concatenate([xp, xp], axis=0), jnp.zeros((), BF16))
                yd.append(_dot(gp, x2))
            y_parts.append(jnp.concatenate(yd, axis=1) + y_off)
            upd = lax.dot_general(bg, xd[:, gl], (((0,), (0,)), ((), ())), preferred_element_type=F32)
            st_ref[g] = st_g * cdec[:, gl] + upd
        y = jnp.concatenate(y_parts, axis=1) + dskip_ref[...] * xs
        y = y * _silu(z_ref[pl.ds(r0, CHUNK), :].astype(F32))
        outs = []
        gw = HPG * HEADDIM
        for g in range(GROUPS):
            v = y[:, g * gw:(g + 1) * gw]
            ms = jnp.mean(v * v, axis=-1, keepdims=True)
            outs.append(v * lax.rsqrt(ms + EPS))
        yn = jnp.concatenate(outs, axis=1) * ng_ref[...]
        yn_ref[pl.ds(r0, CHUNK), :] = yn.astype(BF16)
        return carry

    lax.fori_loop(0, SSD_NC, chunk_body, 0)
    o_ref[...] = _dot(yn_ref[...], wo_ref[...])


def _ssd_branch(p, dt_raw, conv_w, conv_b, dt_bias, a_log, dskip_row, norm_g, w_out, e_mat, tril, mgt, trilt):
    s = p.shape[0]
    tb = SSD_TB
    hb = tb // SSD_HALO
    row = lambda i: (0, 0)
    return pl.pallas_call(
        _ssd_kernel,
        grid=(s // tb,),
        in_specs=[pl.BlockSpec((tb, D_XBC), lambda i: (i, 1)),
                  pl.BlockSpec((SSD_HALO, D_XBC), lambda i: (jnp.maximum(i * hb - 1, 0), 1)),
                  pl.BlockSpec((tb, LANES), lambda i: (i, 0)),
                  pl.BlockSpec((tb, D_INNER), lambda i: (i, 0)),
                  pl.BlockSpec((8, D_XBC), row),
                  pl.BlockSpec((1, D_XBC), row),
                  pl.BlockSpec((1, LANES), row),
                  pl.BlockSpec((1, LANES), row),
                  pl.BlockSpec((1, D_INNER), row),
                  pl.BlockSpec((1, D_INNER), row),
                  pl.BlockSpec((D_INNER, D_MODEL), row),
                  pl.BlockSpec((LANES, D_INNER), row),
                  pl.BlockSpec((CHUNK, CHUNK), row),
                  pl.BlockSpec((CHUNK, D_INNER), row),
                  pl.BlockSpec((CHUNK, D_INNER), row)],
        out_specs=pl.BlockSpec((tb, D_MODEL), lambda i: (i, 0)),
        out_shape=jax.ShapeDtypeStruct((s, D_MODEL), F32),
        scratch_shapes=[pltpu.VMEM((SSD_HALO + tb, D_XBC), F32),
                        pltpu.VMEM((tb, D_XBC), F32),
                        pltpu.VMEM((tb, D_INNER), BF16),
                        pltpu.VMEM((GROUPS, STATE, HPG * HEADDIM), F32)],
        compiler_params=_cparams(("arbitrary",)),
        name="ssd",
    )(p, p, dt_raw, p, conv_w, conv_b, dt_bias, a_log, dskip_row, norm_g, w_out, e_mat, tril, mgt, trilt)


MG_TM = 512


def _merge_kernel(ycv_ref, yssm_ref, gcv_ref, gssm_ref, x_ref, g1_ref, wmix_ref, n2g_ref, sc_ref, sh_ref,
                  wrh_ref, wrl_ref, br_ref, h_ref, n2_ref, idx_ref, wts_ref):
    m = (_sigmoid(gcv_ref[...].astype(F32)) * ycv_ref[...]
         + _sigmoid(gssm_ref[...].astype(F32)) * yssm_ref[...])
    mo = _dot(m.astype(BF16), wmix_ref[...])
    h = x_ref[...] + g1_ref[...] * mo
    h_ref[...] = h
    ms = jnp.mean(h * h, axis=-1, keepdims=True)
    n2 = (h * lax.rsqrt(ms + EPS) * n2g_ref[...]) * (1.0 + sc_ref[...]) + sh_ref[...]
    n2_ref[...] = n2
    hi = n2.astype(BF16)
    lo = (n2 - hi.astype(F32)).astype(BF16)
    logits = _dot(hi, wrh_ref[...]) + _dot(lo, wrh_ref[...]) + _dot(hi, wrl_ref[...]) + br_ref[...]
    lane = lax.broadcasted_iota(I32, logits.shape, 1)
    lane_f = lane.astype(F32)
    neg = jnp.float32(-jnp.inf)
    big = jnp.float32(1e9)
    is_grp = (lane >= N_EXP) & (lane < N_EXP + N_GRP)
    gl = jnp.where(is_grp, logits, neg)
    gmax = jnp.max(gl, axis=-1, keepdims=True)
    gsum = jnp.sum(jnp.where(is_grp, jnp.exp(gl - gmax), 0.0), axis=-1, keepdims=True)
    g_p = 1.0 / gsum
    g_lane = jnp.min(jnp.where(gl == gmax, lane_f, big), axis=-1, keepdims=True)
    g_idx = g_lane.astype(I32) - N_EXP
    in_grp = (lane >= g_idx * EPG) & (lane < g_idx * EPG + EPG)
    el = jnp.where(in_grp, logits, neg)
    m1 = jnp.max(el, axis=-1, keepdims=True)
    i1 = jnp.min(jnp.where(el == m1, lane_f, big), axis=-1, keepdims=True)
    el2 = jnp.where(lane_f == i1, neg, el)
    m2 = jnp.max(el2, axis=-1, keepdims=True)
    i2 = jnp.min(jnp.where(el2 == m2, lane_f, big), axis=-1, keepdims=True)
    r = jnp.exp(m2 - m1)
    w0 = g_p / (1.0 + r)
    w1 = g_p * r / (1.0 + r)
    idx_ref[...] = jnp.where(lane == 0, i1.astype(I32), jnp.where(lane == 1, i2.astype(I32), 0))
    wts_ref[...] = jnp.where(lane == 0, w0, jnp.where(lane == 1, w1, 0.0))


def _merge(ycv, yssm, p, x2, g1, w_mix, n2g, sc2, sh2, wr_hi, wr_lo, b_r):
    s, d = x2.shape
    tm = min(MG_TM, s)
    row = lambda i: (0, 0)
    blk = lambda i: (i, 0)
    return pl.pallas_call(
        _merge_kernel,
        grid=(s // tm,),
        in_specs=[pl.BlockSpec((tm, d), blk), pl.BlockSpec((tm, d), blk),
                  pl.BlockSpec((tm, d), lambda i: (i, 7)), pl.BlockSpec((tm, d), lambda i: (i, 8)),
                  pl.BlockSpec((tm, d), blk),
                  pl.BlockSpec((1, d), row),
                  pl.BlockSpec((d, d), row),
                  pl.BlockSpec((1, d), row), pl.BlockSpec((1, d), row), pl.BlockSpec((1, d), row),
                  pl.BlockSpec((d, LANES), row), pl.BlockSpec((d, LANES), row), pl.BlockSpec((1, LANES), row)],
        out_specs=[pl.BlockSpec((tm, d), blk), pl.BlockSpec((tm, d), blk),
                   pl.BlockSpec((tm, LANES), blk), pl.BlockSpec((tm, LANES), blk)],
        out_shape=[jax.ShapeDtypeStruct((s, d), F32), jax.ShapeDtypeStruct((s, d), F32),
                   jax.ShapeDtypeStruct((s, LANES), I32), jax.ShapeDtypeStruct((s, LANES), F32)],
        compiler_params=_cparams(("arbitrary",)),
        name="merge",
    )(ycv, yssm, p, p, x2, g1, w_mix, n2g, sc2, sh2, wr_hi, wr_lo, b_r)


PLAN_TB = 512


def _plan_kernel(idx_ref, stril_ref, utri_ref, dest_ref, blk_ref, cnt_ref, run_ref, pstart_ref):
    ph = pl.program_id(0)
    i = pl.program_id(1)
    idx = idx_ref[...]
    lane = lax.broadcasted_iota(I32, idx.shape, 1)
    e0 = idx[:, 0:1]
    e1 = idx[:, 1:2]
    oh0 = (lane == e0).astype(F32)
    oh1 = (lane == e1).astype(F32)
    ohs = oh0 + oh1

    @pl.when((ph == 0) & (i == 0))
    def _():
        cnt_ref[...] = jnp.zeros_like(cnt_ref)

    @pl.when(ph == 0)
    def _():
        cnt_ref[...] += jnp.sum(ohs, axis=0, keepdims=True)

    @pl.when((ph == 1) & (i == 0))
    def _():
        cnt = cnt_ref[...]
        units = jnp.floor((cnt + (ROW_BLK - 1)) * (1.0 / ROW_BLK))
        u8 = jnp.broadcast_to(units, (8, LANES)).astype(BF16)
        pend = _dot(u8, utri_ref[...])
        pstart_ref[...] = (pend[0:1] - units) * ROW_BLK
        run_ref[...] = jnp.zeros_like(run_ref)
        nb = blk_ref.shape[0]
        b = lax.broadcasted_iota(I32, (nb, LANES), 0).astype(F32)
        ln = lax.broadcasted_iota(I32, (nb, LANES), 1)
        le = ((pend[0:1] <= b) & (ln < N_EXP)).astype(F32)
        be = jnp.minimum(jnp.sum(le, axis=-1, keepdims=True), N_EXP - 1)
        total = jnp.max(pend[0:1], axis=-1, keepdims=True)
        blk_ref[...] = jnp.where(ln == 0, be, jnp.where(ln == 1, total, 0.0)).astype(I32)

    @pl.when(ph == 1)
    def _():
        prefix = _dot(stril_ref[...], ohs.astype(BF16))
        base = prefix + run_ref[...] + pstart_ref[...]
        d0 = jnp.sum(oh0 * base, axis=-1, keepdims=True)
        d1 = jnp.sum(oh1 * base, axis=-1, keepdims=True)
        dest_ref[...] = jnp.where(lane == 0, d0, jnp.where(lane == 1, d1, 0.0)).astype(I32)
        run_ref[...] += jnp.sum(ohs, axis=0, keepdims=True)


def _plan(idx, stril, utri, n_blocks_pad):
    s = idx.shape[0]
    tb = min(PLAN_TB, s)
    row = lambda ph, i: (0, 0)
    return pl.pallas_call(
        _plan_kernel,
        grid=(2, s // tb),
        in_specs=[pl.BlockSpec((tb, LANES), lambda ph, i: (i, 0)),
                  pl.BlockSpec((tb, tb), row),
                  pl.BlockSpec((LANES, LANES), row)],
        out_specs=[pl.BlockSpec((tb, LANES), lambda ph, i: (i * ph, 0)),
                   pl.BlockSpec((n_blocks_pad, LANES), row)],
        out_shape=[jax.ShapeDtypeStruct((s, LANES), I32),
                   jax.ShapeDtypeStruct((n_blocks_pad, LANES), I32)],
        scratch_shapes=[pltpu.VMEM((1, LANES), F32), pltpu.VMEM((1, LANES), F32), pltpu.VMEM((1, LANES), F32)],
        compiler_params=_cparams(("arbitrary", "arbitrary")),
        name="plan",
    )(idx, stril, utri)


DSP_TB = 256


def _dispatch_kernel(dest_ref, n2_ref, xs_in_ref, xs_ref, sem):
    del xs_in_ref
    i = pl.program_id(0)
    n = pl.num_programs(0)
    na = 2 * DSP_TB

    def row_copy(a, slot):
        t = i * DSP_TB + a // 2
        d = dest_ref[0, 0, a]
        return pltpu.make_async_copy(n2_ref.at[pl.ds(t, 1)], xs_ref.at[pl.ds(d, 1)], sem.at[slot])

    def wait_all(slot):
        def body(a, c):
            pltpu.make_async_copy(n2_ref.at[pl.ds(0, 1)], xs_ref.at[pl.ds(0, 1)], sem.at[slot]).wait()
            return c
        lax.fori_loop(0, na, body, 0)

    slot = i % 2

    def start_body(a, c):
        row_copy(a, slot).start()
        return c
    lax.fori_loop(0, na, start_body, 0)

    @pl.when(i > 0)
    def _():
        wait_all(1 - slot)

    @pl.when(i == n - 1)
    def _():
        wait_all(slot)


def _dispatch(dest_flat, n2, xs_init):
    s, d = n2.shape
    tb = min(DSP_TB, s)
    assert tb == DSP_TB
    return pl.pallas_call(
        _dispatch_kernel,
        grid=(s // tb,),
        in_specs=[pl.BlockSpec((1, 1, 2 * tb), lambda i: (i, 0, 0), memory_space=pltpu.SMEM),
                  pl.BlockSpec(memory_space=pl.ANY),
                  pl.BlockSpec(memory_space=pl.ANY)],
        out_specs=pl.BlockSpec(memory_space=pl.ANY),
        out_shape=jax.ShapeDtypeStruct(xs_init.shape, xs_init.dtype),
        scratch_shapes=[pltpu.SemaphoreType.DMA((2,))],
        input_output_aliases={2: 0},
        compiler_params=_cparams(("arbitrary",)),
        name="dispatch",
    )(dest_flat, n2, xs_init)


def _experts_kernel(be_ref, nu_ref, xs_ref, w1_ref, w3_ref, w2_ref, y_ref, w1b_ref, w3b_ref, w2b_ref):
    b = pl.program_id(0)
    used = b < nu_ref[0]
    prev = be_ref[jnp.maximum(b - 1, 0)]
    fresh = (b == 0) | (be_ref[b] != prev)

    @pl.when(used & fresh)
    def _():
        w1b_ref[...] = w1_ref[0].astype(BF16)
        w3b_ref[...] = w3_ref[0].astype(BF16)
        w2b_ref[...] = w2_ref[0].astype(BF16)

    @pl.when(used)
    def _():
        x = xs_ref[...].astype(BF16)
        h1 = _dot(x, w1b_ref[...])
        h3 = _dot(x, w3b_ref[...])
        hdn = (_silu(h1) * h3).astype(BF16)
        y_ref[...] = _dot(hdn, w2b_ref[...])

    @pl.when(jnp.logical_not(used))
    def _():
        y_ref[...] = jnp.zeros_like(y_ref)


def _experts(blk_e, n_used, xs, w1, w3, w2):
    rows, d = xs.shape
    nb = rows // ROW_BLK

    def xmap(b, be, nu):
        return (jnp.minimum(b, jnp.maximum(nu[0] - 1, 0)), 0)

    def wmap(b, be, nu):
        return (be[jnp.minimum(b, jnp.maximum(nu[0] - 1, 0))], 0, 0)

    grid_spec = pltpu.PrefetchScalarGridSpec(
        num_scalar_prefetch=2,
        grid=(nb,),
        in_specs=[pl.BlockSpec((ROW_BLK, d), xmap),
                  pl.BlockSpec((1, d, D_FF), wmap),
                  pl.BlockSpec((1, d, D_FF), wmap),
                  pl.BlockSpec((1, D_FF, d), wmap)],
        out_specs=pl.BlockSpec((ROW_BLK, d), lambda b, be, nu: (b, 0)),
        scratch_shapes=[pltpu.VMEM((d, D_FF), BF16), pltpu.VMEM((d, D_FF), BF16), pltpu.VMEM((D_FF, d), BF16)],
    )
    return pl.pallas_call(
        _experts_kernel,
        grid_spec=grid_spec,
        out_shape=jax.ShapeDtypeStruct((rows, d), F32),
        compiler_params=_cparams(("arbitrary",)),
        name="experts",
    )(blk_e, n_used, xs, w1, w3, w2)


CMB_TB = 256


def _combine_kernel(dcur_ref, dnxt_ref, yb_ref, h_ref, wts_ref, g2_ref, fg_ref, o_ref, buf_ref, sem):
    i = pl.program_id(0)
    n = pl.num_programs(0)
    na = 2 * CMB_TB

    def issue(dref, slot):
        def body(a, c):
            d = dref[0, 0, a]
            pltpu.make_async_copy(yb_ref.at[pl.ds(d, 1)],
                                  buf_ref.at[slot, a % 2, pl.ds(a // 2, 1)],
                                  sem.at[slot]).start()
            return c
        lax.fori_loop(0, na, body, 0)

    slot = i % 2

    @pl.when(i == 0)
    def _():
        issue(dcur_ref, 0)

    @pl.when(i + 1 < n)
    def _():
        issue(dnxt_ref, 1 - slot)

    def wbody(a, c):
        pltpu.make_async_copy(yb_ref.at[pl.ds(0, 1)], buf_ref.at[slot, 0, pl.ds(0, 1)], sem.at[slot]).wait()
        return c
    lax.fori_loop(0, na, wbody, 0)

    w = wts_ref[...]
    moe = w[:, 0:1] * buf_ref[slot, 0] + w[:, 1:2] * buf_ref[slot, 1]
    h = h_ref[...] + g2_ref[...] * moe
    ms = jnp.mean(h * h, axis=-1, keepdims=True)
    o_ref[...] = h * lax.rsqrt(ms + EPS) * fg_ref[...]


def _combine(dest_flat, yb, h, wts, g2, fg):
    s, d = h.shape
    tb = min(CMB_TB, s)
    assert tb == CMB_TB
    nsteps = s // tb
    row = lambda i: (0, 0)
    blk = lambda i: (i, 0)
    return pl.pallas_call(
        _combine_kernel,
        grid=(nsteps,),
        in_specs=[pl.BlockSpec((1, 1, 2 * tb), lambda i: (i, 0, 0), memory_space=pltpu.SMEM),
                  pl.BlockSpec((1, 1, 2 * tb), lambda i: (jnp.minimum(i + 1, nsteps - 1), 0, 0), memory_space=pltpu.SMEM),
                  pl.BlockSpec(memory_space=pl.ANY),
                  pl.BlockSpec((tb, d), blk),
                  pl.BlockSpec((tb, LANES), blk),
                  pl.BlockSpec((1, d), row), pl.BlockSpec((1, d), row)],
        out_specs=pl.BlockSpec((tb, d), blk),
        out_shape=jax.ShapeDtypeStruct((s, d), F32),
        scratch_shapes=[pltpu.VMEM((2, 2, tb, d), F32), pltpu.SemaphoreType.DMA((2,))],
        compiler_params=_cparams(("arbitrary",)),
        name="combine",
    )(dest_flat, dest_flat, yb, h, wts, g2, fg)


def _pad_rows(a, rows):
    return jnp.pad(a, ((0, rows - a.shape[0]), (0, 0)))


def _pad_lanes(a, lanes):
    return jnp.pad(a, ((0, 0), (0, lanes - a.shape[1])))


def kernel(x, c, w_ada, b_ada, norm1_g, w_in, cv_dw_w, cv_dw_b, cv_ln_g, cv_ln_b, w_cv_out, ssm_conv_w,
           ssm_conv_b, dt_bias, a_log, d_skip, ssm_norm_g, w_ssm_out, w_mix_out, norm2_g, w_grp, b_grp, w_er,
           b_er, w1, w3, w2, final_g):
    bsz, seq, d = x.shape
    assert bsz == 1 and d == D_MODEL and w_ada.shape[0] == 1
    x2 = x.reshape(seq, d)

    mod = _ada(c.reshape(d, 1), w_ada[0], b_ada[0].reshape(1, -1))
    sh1, sc1, g1, sh2, sc2, g2 = [mod[:, k * d:(k + 1) * d] for k in range(6)]

    wi = w_in[0]
    o_glu, o_z, o_xbc, o_dt, o_gate = 0, 2 * d, 2 * d + D_INNER, 2 * d + D_INNER + D_XBC, 2 * d + D_INNER + D_XBC + HEADS
    w_main = jnp.concatenate([wi[:, o_z:o_z + D_INNER], wi[:, o_glu:o_glu + d], wi[:, o_xbc:o_xbc + D_XBC],
                              wi[:, o_glu + d:o_glu + 2 * d], wi[:, o_gate:o_gate + 2 * d]], axis=1).astype(BF16)
    w_dt = _pad_lanes(wi[:, o_dt:o_dt + HEADS], LANES).astype(BF16)
    row = lambda v: v.reshape(1, -1)

    p, dt_raw = _inproj(x2, row(norm1_g[0]), sc1, sh1, w_main, w_dt, F32)

    ycv = _conv_branch(p, _pad_rows(cv_dw_w[0], 32), row(cv_dw_b[0]), row(cv_ln_g[0]), row(cv_ln_b[0]),
                       w_cv_out[0].astype(BF16))

    hh = jnp.arange(LANES)[:, None]
    jj = jnp.arange(D_INNER)[None, :]
    e_mat = ((jj // HEADDIM) == hh).astype(BF16)
    li = jnp.arange(CHUNK)[:, None]
    tril = (jnp.arange(CHUNK)[None, :] <= li).astype(BF16)
    sj = (jj % HEADDIM)
    mgt = (li > sj).astype(F32)
    trilt = (sj <= li).astype(F32)
    yssm = _ssd_branch(p, dt_raw, _pad_rows(ssm_conv_w[0], 8), row(ssm_conv_b[0]),
                       _pad_lanes(row(dt_bias[0]), LANES), _pad_lanes(row(a_log[0]), LANES),
                       row(jnp.repeat(d_skip[0], HEADDIM)), row(ssm_norm_g[0]), w_ssm_out[0].astype(BF16),
                       e_mat, tril, mgt, trilt)

    w_r = _pad_lanes(jnp.concatenate([w_er[0], w_grp[0]], axis=1), LANES)
    wr_hi = w_r.astype(BF16)
    wr_lo = (w_r - wr_hi.astype(F32)).astype(BF16)
    b_r = _pad_lanes(row(jnp.concatenate([b_er[0], b_grp[0]])), LANES)
    h, n2, idx, wts = _merge(ycv, yssm, p, x2, g1, w_mix_out[0].astype(BF16), row(norm2_g[0]), sc2, sh2,
                             wr_hi, wr_lo, b_r)

    n_asg = 2 * seq
    n_blocks = (n_asg + N_EXP * (ROW_BLK - 1)) // ROW_BLK
    nb_pad = -(-n_blocks // 8) * 8
    tp = min(PLAN_TB, seq)
    stril = (jnp.arange(tp)[None, :] < jnp.arange(tp)[:, None]).astype(BF16)
    utri = (jnp.arange(LANES)[:, None] <= jnp.arange(LANES)[None, :]).astype(BF16)
    dest, blk = _plan(idx, stril, utri, nb_pad)
    dest_flat = dest[:, 0:2].reshape(seq // DSP_TB, 1, 2 * DSP_TB)
    blk_e = blk[:n_blocks, 0]
    n_used = blk[0:1, 1]

    xs = _dispatch(dest_flat, n2, jnp.zeros((n_blocks * ROW_BLK, d), F32))
    yb = _experts(blk_e, n_used, xs, w1[0], w3[0], w2[0])
    out = _combine(dest_flat, yb, h, wts, g2, row(final_g))
    return out.reshape(bsz, seq, d)
```

```python
import functools
import math

import jax
import jax.numpy as jnp
from jax import lax
from jax.experimental import pallas as pl
from jax.experimental.pallas import tpu as pltpu

F32 = jnp.float32
BF16 = jnp.bfloat16
I32 = jnp.int32

D_MODEL = 1024
CHUNK = 64
EPS = 1e-6
CONV_K = 31
D_INNER = 2048
HEADDIM = 64
HEADS = 32
GROUPS = 4
HPG = 8
STATE = 128
SSM_CONV_K = 4
D_XBC = D_INNER + 2 * GROUPS * STATE
N_GRP = 8
EPG = 8
N_EXP = 64
D_FF = 512
LANES = 128

P_COLS = 9 * D_MODEL

ROW_BLK = 256
VMEM_LIMIT = 56 * 2**20


def _cparams(sem):
    return pltpu.CompilerParams(dimension_semantics=sem, vmem_limit_bytes=VMEM_LIMIT)


def _sigmoid(x):
    return jax.nn.sigmoid(x)


def _silu(x):
    return x * jax.nn.sigmoid(x)


def _split3(x):
    hi = x.astype(BF16)
    r1 = x - hi.astype(F32)
    mid = r1.astype(BF16)
    lo = (r1 - mid.astype(F32)).astype(BF16)
    return hi, mid, lo


def _dot(a, b):
    return jnp.dot(a, b, preferred_element_type=F32)


def _ada_kernel(c_ref, w_ref, b_ref, o_ref):
    c = c_ref[...]
    s = _silu(c)
    o_ref[...] = jnp.sum(s * w_ref[...], axis=0, keepdims=True) + b_ref[...]


def _ada(c_col, w_ada, b_ada):
    d, n = w_ada.shape
    tn = 512
    return pl.pallas_call(
        _ada_kernel,
        grid=(n // tn,),
        in_specs=[pl.BlockSpec((d, 1), lambda j: (0, 0)),
                  pl.BlockSpec((d, tn), lambda j: (0, j)),
                  pl.BlockSpec((1, tn), lambda j: (0, j))],
        out_specs=pl.BlockSpec((1, tn), lambda j: (0, j)),
        out_shape=jax.ShapeDtypeStruct((1, n), F32),
        compiler_params=_cparams(("arbitrary",)),
        name="ada",
    )(c_col, w_ada, b_ada)


def _inproj_kernel(x_ref, g_ref, sc_ref, sh_ref, w_ref, wdt_ref, p_ref, dt_ref, n1_ref):
    j = pl.program_id(1)

    @pl.when(j == 0)
    def _():
        x = x_ref[...]
        ms = jnp.mean(x * x, axis=-1, keepdims=True)
        y = x * lax.rsqrt(ms + EPS) * g_ref[...]
        n1 = (y * (1.0 + sc_ref[...]) + sh_ref[...]).astype(BF16)
        n1_ref[...] = n1
        dt_ref[...] = _dot(n1, wdt_ref[...])

    p_ref[...] = _dot(n1_ref[...], w_ref[...]).astype(p_ref.dtype)


def _inproj(x2, g, sc, sh, w_main, w_dt, p_dtype):
    s, d = x2.shape
    tm, tn = 1024, 1024
    tm = min(tm, s)
    row = lambda i, j: (0, 0)
    return pl.pallas_call(
        _inproj_kernel,
        grid=(s // tm, P_COLS // tn),
        in_specs=[pl.BlockSpec((tm, d), lambda i, j: (i, 0)),
                  pl.BlockSpec((1, d), row), pl.BlockSpec((1, d), row), pl.BlockSpec((1, d), row),
                  pl.BlockSpec((d, tn), lambda i, j: (0, j)),
                  pl.BlockSpec((d, LANES), row)],
        out_specs=[pl.BlockSpec((tm, tn), lambda i, j: (i, j)),
                   pl.BlockSpec((tm, LANES), lambda i, j: (i, 0))],
        out_shape=[jax.ShapeDtypeStruct((s, P_COLS), p_dtype),
                   jax.ShapeDtypeStruct((s, LANES), F32)],
        scratch_shapes=[pltpu.VMEM((tm, d), BF16)],
        compiler_params=_cparams(("arbitrary", "arbitrary")),
        name="inproj",
    )(x2, g, sc, sh, w_main, w_dt)


CV_TB = 256
CV_HALO = 32
CV_RC = 64
CV_LC = 128
SUBLANES = 8


def _conv_kernel(a_ref, b_ref, ah_ref, bh_ref, w_ref, cb_ref, lg_ref, lb_ref, wo_ref, o_ref, u_ref, v_ref):
    i = pl.program_id(0)
    u_ref[CV_HALO:, :] = a_ref[...].astype(F32) * _sigmoid(b_ref[...].astype(F32))
    uh = ah_ref[...].astype(F32) * _sigmoid(bh_ref[...].astype(F32))
    u_ref[0:CV_HALO, :] = jnp.where(i > 0, uh, 0.0)
    off0 = CV_HALO - (CONV_K - 1)
    win = CV_RC + CV_HALO
    for r in range(CV_TB // CV_RC):
        for c in range(D_MODEL // CV_LC):
            lanes = slice(c * CV_LC, (c + 1) * CV_LC)
            window = u_ref[r * CV_RC:r * CV_RC + win, lanes]
            acc = jnp.zeros((CV_RC, CV_LC), F32)
            for b in range(SUBLANES):
                shifted = window if b == 0 else pltpu.roll(window, win - b, axis=0)
                for k in range(CONV_K):
                    if (off0 + k) % SUBLANES != b:
                        continue
                    a8 = off0 + k - b
                    acc = acc + w_ref[k:k + 1, lanes] * shifted[a8:a8 + CV_RC, :]
            v_ref[r * CV_RC:(r + 1) * CV_RC, lanes] = acc
    v = v_ref[...] + cb_ref[...]
    mu = jnp.mean(v, axis=-1, keepdims=True)
    vc = v - mu
    var = jnp.mean(vc * vc, axis=-1, keepdims=True)
    y = vc * lax.rsqrt(var + EPS) * lg_ref[...] + lb_ref[...]
    o_ref[...] = _dot(_silu(y).astype(BF16), wo_ref[...]).astype(o_ref.dtype)


def _conv_branch(p, cv_w, cv_b, ln_g, ln_b, w_out):
    s = p.shape[0]
    tb = CV_TB
    hb = tb // CV_HALO
    d = D_MODEL
    row = lambda i: (0, 0)
    halo = lambda col: (lambda i: (jnp.maximum(i * hb - 1, 0), col))
    return pl.pallas_call(
        _conv_kernel,
        grid=(s // tb,),
        in_specs=[pl.BlockSpec((tb, d), lambda i: (i, 2)),
                  pl.BlockSpec((tb, d), lambda i: (i, 6)),
                  pl.BlockSpec((CV_HALO, d), halo(2)),
                  pl.BlockSpec((CV_HALO, d), halo(6)),
                  pl.BlockSpec((32, d), row),
                  pl.BlockSpec((1, d), row), pl.BlockSpec((1, d), row), pl.BlockSpec((1, d), row),
                  pl.BlockSpec((d, d), row)],
        out_specs=pl.BlockSpec((tb, d), lambda i: (i, 0)),
        out_shape=jax.ShapeDtypeStruct((s, d), BF16),
        scratch_shapes=[pltpu.VMEM((CV_HALO + tb, d), F32), pltpu.VMEM((tb, d), F32)],
        compiler_params=_cparams(("arbitrary",)),
        name="conv",
    )(p, p, p, p, cv_w, cv_b, ln_g, ln_b, w_out)


SSD_TB = 256
SSD_NC = SSD_TB // CHUNK
SSD_HALO = 16
SSD_LC = 512


def _ssd_kernel(xbc_ref, xbch_ref, dt_ref, z_ref, cw_ref, cbias_ref, dtb_ref, alog_ref, dskip_ref,
                ng_ref, wo_ref, e_ref, tril_ref, mgt_ref, trilt_ref, o_ref,
                xc_ref, xconv_ref, yn_ref, st_ref):
    i = pl.program_id(0)

    @pl.when(i == 0)
    def _():
        st_ref[...] = jnp.zeros_like(st_ref)

    xc_ref[SSD_HALO:, :] = xbc_ref[...].astype(F32)
    xc_ref[0:SSD_HALO, :] = jnp.where(i > 0, xbch_ref[...].astype(F32), 0.0)
    win = CHUNK + SUBLANES
    for r in range(SSD_NC):
        for c in range(D_XBC // SSD_LC):
            lanes = slice(c * SSD_LC, (c + 1) * SSD_LC)
            r0 = SSD_HALO + r * CHUNK - SUBLANES
            window = xc_ref[r0:r0 + win, lanes]
            acc = cbias_ref[:, lanes] + cw_ref[SSM_CONV_K - 1:SSM_CONV_K, lanes] * window[SUBLANES:, :]
            for sft in range(1, SSM_CONV_K):
                k = SSM_CONV_K - 1 - sft
                shifted = pltpu.roll(window, sft, axis=0)
                acc = acc + cw_ref[k:k + 1, lanes] * shifted[SUBLANES:, :]
            xconv_ref[r * CHUNK:(r + 1) * CHUNK, lanes] = _silu(acc)

    a_row = -jnp.exp(alog_ref[...])
    e_mat = e_ref[...]
    tril = tril_ref[...]
    rr = lax.broadcasted_iota(I32, (2 * CHUNK, LANES), 0)
    cc = lax.broadcasted_iota(I32, (2 * CHUNK, LANES), 1)
    pair_mask = (rr < CHUNK) == (cc < HEADDIM)

    def chunk_body(c, carry):
        r0 = pl.multiple_of(c * CHUNK, CHUNK)
        xs = xconv_ref[pl.ds(r0, CHUNK), 0:D_INNER]
        bm = xconv_ref[pl.ds(r0, CHUNK), D_INNER:D_INNER + GROUPS * STATE].astype(BF16)
        cm = xconv_ref[pl.ds(r0, CHUNK), D_INNER + GROUPS * STATE:D_XBC].astype(BF16)
        dtr = dt_ref[pl.ds(r0, CHUNK), :] + dtb_ref[...]
        dt = jnp.maximum(dtr, 0.0) + jnp.log(1.0 + jnp.exp(-jnp.abs(dtr)))
        ad = dt * a_row
        st = jnp.concatenate([ad, dt], axis=0)
        h3 = _split3(st)
        ex = _dot(h3[0], e_mat) + _dot(h3[1], e_mat) + _dot(h3[2], e_mat)
        adb = ex[0:CHUNK]
        dtb = ex[CHUNK:2 * CHUNK]
        w = jnp.concatenate([adb, adb * mgt_ref[...]], axis=1)
        w3 = _split3(w)
        cs = _dot(tril, w3[0]) + _dot(tril, w3[1]) + _dot(tril, w3[2])
        acs = cs[:, 0:D_INNER]
        seg = cs[:, D_INNER:2 * D_INNER]
        tot = acs[CHUNK - 1:CHUNK, :]
        e_acs = jnp.exp(acs)
        decay = jnp.exp(tot - acs)
        cdec = jnp.exp(tot)
        lmat = jnp.where(trilt_ref[...] > 0.0, jnp.exp(seg), 0.0)
        xdt = xs * dtb
        xd = (xdt * decay).astype(BF16)
        xdt_b = xdt.astype(BF16)
        y_parts = []
        for g in range(GROUPS):
            cg = cm[:, g * STATE:(g + 1) * STATE]
            bg = bm[:, g * STATE:(g + 1) * STATE]
            b2 = jnp.concatenate([bg, bg], axis=0)
            cb2 = lax.dot_general(cg, b2, (((1,), (1,)), ((), ())), preferred_element_type=F32)
            gl = slice(g * HPG * HEADDIM, (g + 1) * HPG * HEADDIM)
            st_g = st_ref[g]
            y_off = _dot(cg, st_g.astype(BF16)) * e_acs[:, gl]
            yd = []
            for k in range(HPG // 2):
                pl_ = slice(g * HPG * HEADDIM + k * LANES, g * HPG * HEADDIM + (k + 1) * LANES)
                gp = (cb2 * lmat[:, pl_]).astype(BF16)
                xp = xdt_b[:, pl_]
                x2 = jnp.where(pair_mask, jnp.concatenate([xp, xp], axis=0), jnp.zeros((), BF16))
                yd.append(_dot(gp, x2))
            y_parts.append(jnp.concatenate(yd, axis=1) + y_off)
            upd = lax.dot_general(bg, xd[:, gl], (((0,), (0,)), ((), ())), preferred_element_type=F32)
            st_ref[g] = st_g * cdec[:, gl] + upd
        y = jnp.concatenate(y_parts, axis=1) + dskip_ref[...] * xs
        y = y * _silu(z_ref[pl.ds(r0, CHUNK), :].astype(F32))
        outs = []
        gw = HPG * HEADDIM
        for g in range(GROUPS):
            v = y[:, g * gw:(g + 1) * gw]
            ms = jnp.mean(v * v, axis=-1, keepdims=True)
            outs.append(v * lax.rsqrt(ms + EPS))
        yn = jnp.concatenate(outs, axis=1) * ng_ref[...]
        yn_ref[pl.ds(r0, CHUNK), :] = yn.astype(BF16)
        return carry

    lax.fori_loop(0, SSD_NC, chunk_body, 0)
    o_ref[...] = _dot(yn_ref[...], wo_ref[...]).astype(o_ref.dtype)


def _ssd_branch(p, dt_raw, conv_w, conv_b, dt_bias, a_log, dskip_row, norm_g, w_out, e_mat, tril, mgt, trilt):
    s = p.shape[0]
    tb = SSD_TB
    hb = tb // SSD_HALO
    row = lambda i: (0, 0)
    return pl.pallas_call(
        _ssd_kernel,
        grid=(s // tb,),
        in_specs=[pl.BlockSpec((tb, D_XBC), lambda i: (i, 1)),
                  pl.BlockSpec((SSD_HALO, D_XBC), lambda i: (jnp.maximum(i * hb - 1, 0), 1)),
                  pl.BlockSpec((tb, LANES), lambda i: (i, 0)),
                  pl.BlockSpec((tb, D_INNER), lambda i: (i, 0)),
                  pl.BlockSpec((8, D_XBC), row),
                  pl.BlockSpec((1, D_XBC), row),
                  pl.BlockSpec((1, LANES), row),
                  pl.BlockSpec((1, LANES), row),
                  pl.BlockSpec((1, D_INNER), row),
                  pl.BlockSpec((1, D_INNER), row),
                  pl.BlockSpec((D_INNER, D_MODEL), row),
                  pl.BlockSpec((LANES, D_INNER), row),
                  pl.BlockSpec((CHUNK, CHUNK), row),
                  pl.BlockSpec((CHUNK, D_INNER), row),
                  pl.BlockSpec((CHUNK, D_INNER), row)],
        out_specs=pl.BlockSpec((tb, D_MODEL), lambda i: (i, 0)),
        out_shape=jax.ShapeDtypeStruct((s, D_MODEL), BF16),
        scratch_shapes=[pltpu.VMEM((SSD_HALO + tb, D_XBC), F32),
                        pltpu.VMEM((tb, D_XBC), F32),
                        pltpu.VMEM((tb, D_INNER), BF16),
                        pltpu.VMEM((GROUPS, STATE, HPG * HEADDIM), F32)],
        compiler_params=_cparams(("arbitrary",)),
        name="ssd",
    )(p, p, dt_raw, p, conv_w, conv_b, dt_bias, a_log, dskip_row, norm_g, w_out, e_mat, tril, mgt, trilt)


MG_TM = 512


def _merge_kernel(ycv_ref, yssm_ref, gcv_ref, gssm_ref, x_ref, g1_ref, wmix_ref, n2g_ref, sc_ref, sh_ref,
                  wrh_ref, wrl_ref, br_ref, h_ref, n2_ref, idx_ref, wts_ref):
    tm = x_ref.shape[0]
    d_tiles = x_ref.shape[1] // LANES
    m = (_sigmoid(gcv_ref[...].astype(F32)) * ycv_ref[...].astype(F32)
         + _sigmoid(gssm_ref[...].astype(F32)) * yssm_ref[...].astype(F32))
    mo = _dot(m.astype(BF16), wmix_ref[...])
    h = x_ref[...] + g1_ref[...] * mo
    h_ref[...] = h
    ms = jnp.mean(h * h, axis=-1, keepdims=True)
    n2 = (h * lax.rsqrt(ms + EPS) * n2g_ref[...]) * (1.0 + sc_ref[...]) + sh_ref[...]
    for q in range(d_tiles):
        n2_ref[pl.ds(q, tm, stride=d_tiles), :] = n2[:, q * LANES:(q + 1) * LANES]
    hi = n2.astype(BF16)
    lo = (n2 - hi.astype(F32)).astype(BF16)
    logits = _dot(hi, wrh_ref[...]) + _dot(lo, wrh_ref[...]) + _dot(hi, wrl_ref[...]) + br_ref[...]
    lane = lax.broadcasted_iota(I32, logits.shape, 1)
    lane_f = lane.astype(F32)
    neg = jnp.float32(-jnp.inf)
    big = jnp.float32(1e9)
    is_grp = (lane >= N_EXP) & (lane < N_EXP + N_GRP)
    gl = jnp.where(is_grp, logits, neg)
    gmax = jnp.max(gl, axis=-1, keepdims=True)
    gsum = jnp.sum(jnp.where(is_grp, jnp.exp(gl - gmax), 0.0), axis=-1, keepdims=True)
    g_p = 1.0 / gsum
    g_lane = jnp.min(jnp.where(gl == gmax, lane_f, big), axis=-1, keepdims=True)
    g_idx = g_lane.astype(I32) - N_EXP
    in_grp = (lane >= g_idx * EPG) & (lane < g_idx * EPG + EPG)
    el = jnp.where(in_grp, logits, neg)
    m1 = jnp.max(el, axis=-1, keepdims=True)
    i1 = jnp.min(jnp.where(el == m1, lane_f, big), axis=-1, keepdims=True)
    el2 = jnp.where(lane_f == i1, neg, el)
    m2 = jnp.max(el2, axis=-1, keepdims=True)
    i2 = jnp.min(jnp.where(el2 == m2, lane_f, big), axis=-1, keepdims=True)
    r = jnp.exp(m2 - m1)
    w0 = g_p / (1.0 + r)
    w1 = g_p * r / (1.0 + r)
    idx_ref[...] = jnp.where(lane == 0, i1.astype(I32), jnp.where(lane == 1, i2.astype(I32), 0))
    wts_ref[...] = jnp.where(lane == 0, w0, jnp.where(lane == 1, w1, 0.0))


def _merge(ycv, yssm, p, x2, g1, w_mix, n2g, sc2, sh2, wr_hi, wr_lo, b_r):
    s, d = x2.shape
    tm = min(MG_TM, s)
    row = lambda i: (0, 0)
    blk = lambda i: (i, 0)
    return pl.pallas_call(
        _merge_kernel,
        grid=(s // tm,),
        in_specs=[pl.BlockSpec((tm, d), blk), pl.BlockSpec((tm, d), blk),
                  pl.BlockSpec((tm, d), lambda i: (i, 7)), pl.BlockSpec((tm, d), lambda i: (i, 8)),
                  pl.BlockSpec((tm, d), blk),
                  pl.BlockSpec((1, d), row),
                  pl.BlockSpec((d, d), row),
                  pl.BlockSpec((1, d), row), pl.BlockSpec((1, d), row), pl.BlockSpec((1, d), row),
                  pl.BlockSpec((d, LANES), row), pl.BlockSpec((d, LANES), row), pl.BlockSpec((1, LANES), row)],
        out_specs=[pl.BlockSpec((tm, d), blk), pl.BlockSpec((tm * (d // LANES), LANES), blk),
                   pl.BlockSpec((tm, LANES), blk), pl.BlockSpec((tm, LANES), blk)],
        out_shape=[jax.ShapeDtypeStruct((s, d), F32), jax.ShapeDtypeStruct((s * (d // LANES), LANES), F32),
                   jax.ShapeDtypeStruct((s, LANES), I32), jax.ShapeDtypeStruct((s, LANES), F32)],
        compiler_params=_cparams(("arbitrary",)),
        name="merge",
    )(ycv, yssm, p, p, x2, g1, w_mix, n2g, sc2, sh2, wr_hi, wr_lo, b_r)


PLAN_TB = 512


def _plan_kernel(idx_ref, stril_ref, utri_ref, dest_ref, blk_ref, cnt_ref, run_ref, pstart_ref):
    ph = pl.program_id(0)
    i = pl.program_id(1)
    idx = idx_ref[...]
    lane = lax.broadcasted_iota(I32, idx.shape, 1)
    e0 = idx[:, 0:1]
    e1 = idx[:, 1:2]
    oh0 = (lane == e0).astype(F32)
    oh1 = (lane == e1).astype(F32)
    ohs = oh0 + oh1

    @pl.when((ph == 0) & (i == 0))
    def _():
        cnt_ref[...] = jnp.zeros_like(cnt_ref)

    @pl.when(ph == 0)
    def _():
        cnt_ref[...] += jnp.sum(ohs, axis=0, keepdims=True)

    @pl.when((ph == 1) & (i == 0))
    def _():
        cnt = cnt_ref[...]
        units = jnp.floor((cnt + (ROW_BLK - 1)) * (1.0 / ROW_BLK))
        u8 = jnp.broadcast_to(units, (8, LANES)).astype(BF16)
        pend = _dot(u8, utri_ref[...])
        pstart_ref[...] = (pend[0:1] - units) * ROW_BLK
        run_ref[...] = jnp.zeros_like(run_ref)
        nb = blk_ref.shape[0]
        b = lax.broadcasted_iota(I32, (nb, LANES), 0).astype(F32)
        ln = lax.broadcasted_iota(I32, (nb, LANES), 1)
        le = ((pend[0:1] <= b) & (ln < N_EXP)).astype(F32)
        be = jnp.minimum(jnp.sum(le, axis=-1, keepdims=True), N_EXP - 1)
        total = jnp.max(pend[0:1], axis=-1, keepdims=True)
        blk_ref[...] = jnp.where(ln == 0, be, jnp.where(ln == 1, total, 0.0)).astype(I32)

    @pl.when(ph == 1)
    def _():
        prefix = _dot(stril_ref[...], ohs.astype(BF16))
        base = prefix + run_ref[...] + pstart_ref[...]
        d0 = jnp.sum(oh0 * base, axis=-1, keepdims=True)
        d1 = jnp.sum(oh1 * base, axis=-1, keepdims=True)
        dest_ref[...] = jnp.where(lane == 0, d0, jnp.where(lane == 1, d1, 0.0)).astype(I32)
        run_ref[...] += jnp.sum(ohs, axis=0, keepdims=True)


def _plan(idx, stril, utri, n_blocks_pad):
    s = idx.shape[0]
    tb = min(PLAN_TB, s)
    row = lambda ph, i: (0, 0)
    return pl.pallas_call(
        _plan_kernel,
        grid=(2, s // tb),
        in_specs=[pl.BlockSpec((tb, LANES), lambda ph, i: (i, 0)),
                  pl.BlockSpec((tb, tb), row),
                  pl.BlockSpec((LANES, LANES), row)],
        out_specs=[pl.BlockSpec((tb, LANES), lambda ph, i: (i * ph, 0)),
                   pl.BlockSpec((n_blocks_pad, LANES), row)],
        out_shape=[jax.ShapeDtypeStruct((s, LANES), I32),
                   jax.ShapeDtypeStruct((n_blocks_pad, LANES), I32)],
        scratch_shapes=[pltpu.VMEM((1, LANES), F32), pltpu.VMEM((1, LANES), F32), pltpu.VMEM((1, LANES), F32)],
        compiler_params=_cparams(("arbitrary", "arbitrary")),
        name="plan",
    )(idx, stril, utri)


DSP_TB = 256
ROW_TILES = D_MODEL // LANES


def _dispatch_kernel(dest_ref, n2_ref, xs_in_ref, xs_ref, sem):
    del xs_in_ref
    i = pl.program_id(0)
    n = pl.num_programs(0)
    na = 2 * DSP_TB
    slot = i % 2

    def start_body(a, c):
        t = pl.multiple_of((i * DSP_TB + a // 2) * ROW_TILES, ROW_TILES)
        d = pl.multiple_of(dest_ref[0, 0, a] * ROW_TILES, ROW_TILES)
        pltpu.make_async_copy(n2_ref.at[pl.ds(t, ROW_TILES)], xs_ref.at[pl.ds(d, ROW_TILES)], sem.at[slot]).start()
        return c
    lax.fori_loop(0, na, start_body, 0, unroll=8)

    def wait_all(sl):
        pltpu.make_async_copy(n2_ref.at[pl.ds(0, na * ROW_TILES)], xs_ref.at[pl.ds(0, na * ROW_TILES)],
                              sem.at[sl]).wait()

    @pl.when(i > 0)
    def _():
        wait_all(1 - slot)

    @pl.when(i == n - 1)
    def _():
        wait_all(slot)


def _dispatch(dest_flat, n2, xs_init):
    s = n2.shape[0] // ROW_TILES
    tb = min(DSP_TB, s)
    assert tb == DSP_TB
    return pl.pallas_call(
        _dispatch_kernel,
        grid=(s // tb,),
        in_specs=[pl.BlockSpec((1, 1, 2 * tb), lambda i: (i, 0, 0), memory_space=pltpu.SMEM),
                  pl.BlockSpec(memory_space=pl.ANY),
                  pl.BlockSpec(memory_space=pl.ANY)],
        out_specs=pl.BlockSpec(memory_space=pl.ANY),
        out_shape=jax.ShapeDtypeStruct(xs_init.shape, xs_init.dtype),
        scratch_shapes=[pltpu.SemaphoreType.DMA((2,))],
        input_output_aliases={2: 0},
        compiler_params=_cparams(("arbitrary",)),
        name="dispatch",
    )(dest_flat, n2, xs_init)


def _experts_kernel(be_ref, nu_ref, xs_ref, w1_ref, w3_ref, w2_ref, y_ref, w1b_ref, w3b_ref, w2b_ref):
    b = pl.program_id(0)
    used = b < nu_ref[0]
    prev = be_ref[jnp.maximum(b - 1, 0)]
    fresh = (b == 0) | (be_ref[b] != prev)

    @pl.when(used & fresh)
    def _():
        w1b_ref[...] = w1_ref[0].astype(BF16)
        w3b_ref[...] = w3_ref[0].astype(BF16)
        w2b_ref[...] = w2_ref[0].astype(BF16)

    @pl.when(used)
    def _():
        x = jnp.concatenate([xs_ref[pl.ds(q, ROW_BLK, stride=ROW_TILES), :] for q in range(ROW_TILES)],
                            axis=1).astype(BF16)
        h1 = _dot(x, w1b_ref[...])
        h3 = _dot(x, w3b_ref[...])
        hdn = (_silu(h1) * h3).astype(BF16)
        y = _dot(hdn, w2b_ref[...])
        for q in range(ROW_TILES):
            y_ref[pl.ds(q, ROW_BLK, stride=ROW_TILES), :] = y[:, q * LANES:(q + 1) * LANES]

    @pl.when(jnp.logical_not(used))
    def _():
        y_ref[...] = jnp.zeros_like(y_ref)


def _experts(blk_e, n_used, xs, w1, w3, w2):
    d = w1.shape[1]
    rb = ROW_BLK * ROW_TILES
    nb = xs.shape[0] // rb

    def xmap(b, be, nu):
        return (jnp.minimum(b, jnp.maximum(nu[0] - 1, 0)), 0)

    def wmap(b, be, nu):
        return (be[jnp.minimum(b, jnp.maximum(nu[0] - 1, 0))], 0, 0)

    grid_spec = pltpu.PrefetchScalarGridSpec(
        num_scalar_prefetch=2,
        grid=(nb,),
        in_specs=[pl.BlockSpec((rb, LANES), xmap),
                  pl.BlockSpec((1, d, D_FF), wmap),
                  pl.BlockSpec((1, d, D_FF), wmap),
                  pl.BlockSpec((1, D_FF, d), wmap)],
        out_specs=pl.BlockSpec((rb, LANES), lambda b, be, nu: (b, 0)),
        scratch_shapes=[pltpu.VMEM((d, D_FF), BF16), pltpu.VMEM((d, D_FF), BF16), pltpu.VMEM((D_FF, d), BF16)],
    )
    return pl.pallas_call(
        _experts_kernel,
        grid_spec=grid_spec,
        out_shape=jax.ShapeDtypeStruct(xs.shape, F32),
        compiler_params=_cparams(("arbitrary",)),
        name="experts",
    )(blk_e, n_used, xs, w1, w3, w2)


CMB_TB = 256


def _combine_kernel(dcur_ref, dnxt_ref, yb_ref, h_ref, wts_ref, g2_ref, fg_ref, o_ref, buf_ref, sem):
    i = pl.program_id(0)
    n = pl.num_programs(0)
    na = 2 * CMB_TB
    rows = CMB_TB * ROW_TILES

    def issue(dref, sl):
        def body(a, c):
            d = pl.multiple_of(dref[0, 0, a] * ROW_TILES, ROW_TILES)
            j = pl.multiple_of((a // 2) * ROW_TILES, ROW_TILES)
            pltpu.make_async_copy(yb_ref.at[pl.ds(d, ROW_TILES)], buf_ref.at[sl, a % 2, pl.ds(j, ROW_TILES)],
                                  sem.at[sl]).start()
            return c
        lax.fori_loop(0, na, body, 0, unroll=8)

    slot = i % 2

    @pl.when(i == 0)
    def _():
        issue(dcur_ref, 0)

    @pl.when(i + 1 < n)
    def _():
        issue(dnxt_ref, 1 - slot)

    for k in range(2):
        pltpu.make_async_copy(yb_ref.at[pl.ds(0, rows)], buf_ref.at[slot, k], sem.at[slot]).wait()

    w = wts_ref[...]
    w0 = w[:, 0:1]
    w1 = w[:, 1:2]
    hs = []
    ssq = jnp.zeros((CMB_TB, 1), F32)
    for q in range(ROW_TILES):
        lanes = slice(q * LANES, (q + 1) * LANES)
        moe = (w0 * buf_ref[slot, 0, pl.ds(q, CMB_TB, stride=ROW_TILES), :]
               + w1 * buf_ref[slot, 1, pl.ds(q, CMB_TB, stride=ROW_TILES), :])
        hq = h_ref[:, lanes] + g2_ref[:, lanes] * moe
        hs.append(hq)
        ssq = ssq + jnp.sum(hq * hq, axis=-1, keepdims=True)
    scale = lax.rsqrt(ssq * (1.0 / D_MODEL) + EPS)
    for q in range(ROW_TILES):
        lanes = slice(q * LANES, (q + 1) * LANES)
        o_ref[:, lanes] = hs[q] * scale * fg_ref[:, lanes]


def _combine(dest_flat, yb, h, wts, g2, fg):
    s, d = h.shape
    tb = min(CMB_TB, s)
    assert tb == CMB_TB
    nsteps = s // tb
    row = lambda i: (0, 0)
    blk = lambda i: (i, 0)
    return pl.pallas_call(
        _combine_kernel,
        grid=(nsteps,),
        in_specs=[pl.BlockSpec((1, 1, 2 * tb), lambda i: (i, 0, 0), memory_space=pltpu.SMEM),
                  pl.BlockSpec((1, 1, 2 * tb), lambda i: (jnp.minimum(i + 1, nsteps - 1), 0, 0), memory_space=pltpu.SMEM),
                  pl.BlockSpec(memory_space=pl.ANY),
                  pl.BlockSpec((tb, d), blk),
                  pl.BlockSpec((tb, LANES), blk),
                  pl.BlockSpec((1, d), row), pl.BlockSpec((1, d), row)],
        out_specs=pl.BlockSpec((tb, d), blk),
        out_shape=jax.ShapeDtypeStruct((s, d), F32),
        scratch_shapes=[pltpu.VMEM((2, 2, tb * ROW_TILES, LANES), F32), pltpu.SemaphoreType.DMA((2,))],
        compiler_params=_cparams(("arbitrary",)),
        name="combine",
    )(dest_flat, dest_flat, yb, h, wts, g2, fg)


def _pad_rows(a, rows):
    return jnp.pad(a, ((0, rows - a.shape[0]), (0, 0)))


def _pad_lanes(a, lanes):
    return jnp.pad(a, ((0, 0), (0, lanes - a.shape[1])))


def kernel(x, c, w_ada, b_ada, norm1_g, w_in, cv_dw_w, cv_dw_b, cv_ln_g, cv_ln_b, w_cv_out, ssm_conv_w,
           ssm_conv_b, dt_bias, a_log, d_skip, ssm_norm_g, w_ssm_out, w_mix_out, norm2_g, w_grp, b_grp, w_er,
           b_er, w1, w3, w2, final_g):
    bsz, seq, d = x.shape
    assert bsz == 1 and d == D_MODEL and w_ada.shape[0] == 1
    x2 = x.reshape(seq, d)

    mod = _ada(c.reshape(d, 1), w_ada[0], b_ada[0].reshape(1, -1))
    sh1, sc1, g1, sh2, sc2, g2 = [mod[:, k * d:(k + 1) * d] for k in range(6)]

    wi = w_in[0]
    o_glu, o_z, o_xbc, o_dt, o_gate = 0, 2 * d, 2 * d + D_INNER, 2 * d + D_INNER + D_XBC, 2 * d + D_INNER + D_XBC + HEADS
    w_main = jnp.concatenate([wi[:, o_z:o_z + D_INNER], wi[:, o_glu:o_glu + d], wi[:, o_xbc:o_xbc + D_XBC],
                              wi[:, o_glu + d:o_glu + 2 * d], wi[:, o_gate:o_gate + 2 * d]], axis=1).astype(BF16)
    w_dt = _pad_lanes(wi[:, o_dt:o_dt + HEADS], LANES).astype(BF16)
    row = lambda v: v.reshape(1, -1)

    p, dt_raw = _inproj(x2, row(norm1_g[0]), sc1, sh1, w_main, w_dt, BF16)

    ycv = _conv_branch(p, _pad_rows(cv_dw_w[0], 32), row(cv_dw_b[0]), row(cv_ln_g[0]), row(cv_ln_b[0]),
                       w_cv_out[0].astype(BF16))

    hh = jnp.arange(LANES)[:, None]
    jj = jnp.arange(D_INNER)[None, :]
    e_mat = ((jj // HEADDIM) == hh).astype(BF16)
    li = jnp.arange(CHUNK)[:, None]
    tril = (jnp.arange(CHUNK)[None, :] <= li).astype(BF16)
    sj = (jj % HEADDIM)
    mgt = (li > sj).astype(F32)
    trilt = (sj <= li).astype(F32)
    yssm = _ssd_branch(p, dt_raw, _pad_rows(ssm_conv_w[0], 8), row(ssm_conv_b[0]),
                       _pad_lanes(row(dt_bias[0]), LANES), _pad_lanes(row(a_log[0]), LANES),
                       row(jnp.repeat(d_skip[0], HEADDIM)), row(ssm_norm_g[0]), w_ssm_out[0].astype(BF16),
                       e_mat, tril, mgt, trilt)

    w_r = _pad_lanes(jnp.concatenate([w_er[0], w_grp[0]], axis=1), LANES)
    wr_hi = w_r.astype(BF16)
    wr_lo = (w_r - wr_hi.astype(F32)).astype(BF16)
    b_r = _pad_lanes(row(jnp.concatenate([b_er[0], b_grp[0]])), LANES)
    h, n2, idx, wts = _merge(ycv, yssm, p, x2, g1, w_mix_out[0].astype(BF16), row(norm2_g[0]), sc2, sh2,
                             wr_hi, wr_lo, b_r)

    n_asg = 2 * seq
    n_blocks = (n_asg + N_EXP * (ROW_BLK - 1)) // ROW_BLK
    nb_pad = -(-n_blocks // 8) * 8
    tp = min(PLAN_TB, seq)
    stril = (jnp.arange(tp)[None, :] < jnp.arange(tp)[:, None]).astype(BF16)
    utri = (jnp.arange(LANES)[:, None] <= jnp.arange(LANES)[None, :]).astype(BF16)
    dest, blk = _plan(idx, stril, utri, nb_pad)
    dest_flat = dest[:, 0:2].reshape(seq // DSP_TB, 1, 2 * DSP_TB)
    blk_e = blk[:n_blocks, 0]
    n_used = blk[0:1, 1]

    xs = _dispatch(dest_flat, n2, jnp.zeros((n_blocks * ROW_BLK * ROW_TILES, LANES), F32))
    yb = _experts(blk_e, n_used, xs, w1[0], w3[0], w2[0])
    out = _combine(dest_flat, yb, h, wts, g2, row(final_g))
    return out.reshape(bsz, seq, d)
```

```python
import functools
import math

import jax
import jax.numpy as jnp
from jax import lax
from jax.experimental import pallas as pl
from jax.experimental.pallas import tpu as pltpu

F32 = jnp.float32
BF16 = jnp.bfloat16
I32 = jnp.int32

D_MODEL = 1024
CHUNK = 64
EPS = 1e-6
CONV_K = 31
D_INNER = 2048
HEADDIM = 64
HEADS = 32
GROUPS = 4
HPG = 8
STATE = 128
SSM_CONV_K = 4
D_XBC = D_INNER + 2 * GROUPS * STATE
N_GRP = 8
EPG = 8
N_EXP = 64
D_FF = 512
LANES = 128

P_COLS = 9 * D_MODEL

ROW_BLK = 256
VMEM_LIMIT = 56 * 2**20


def _cparams(sem):
    return pltpu.CompilerParams(dimension_semantics=sem, vmem_limit_bytes=VMEM_LIMIT)


def _sigmoid(x):
    return jax.nn.sigmoid(x)


def _silu(x):
    return x * jax.nn.sigmoid(x)


def _split3(x):
    hi = x.astype(BF16)
    r1 = x - hi.astype(F32)
    mid = r1.astype(BF16)
    lo = (r1 - mid.astype(F32)).astype(BF16)
    return hi, mid, lo


def _dot(a, b):
    return jnp.dot(a, b, preferred_element_type=F32)


def _ada_kernel(c_ref, w_ref, b_ref, o_ref):
    c = c_ref[...]
    s = _silu(c)
    o_ref[...] = jnp.sum(s * w_ref[...], axis=0, keepdims=True) + b_ref[...]


def _ada(c_col, w_ada, b_ada):
    d, n = w_ada.shape
    tn = 512
    return pl.pallas_call(
        _ada_kernel,
        grid=(n // tn,),
        in_specs=[pl.BlockSpec((d, 1), lambda j: (0, 0)),
                  pl.BlockSpec((d, tn), lambda j: (0, j)),
                  pl.BlockSpec((1, tn), lambda j: (0, j))],
        out_specs=pl.BlockSpec((1, tn), lambda j: (0, j)),
        out_shape=jax.ShapeDtypeStruct((1, n), F32),
        compiler_params=_cparams(("arbitrary",)),
        name="ada",
    )(c_col, w_ada, b_ada)


def _inproj_kernel(x_ref, g_ref, sc_ref, sh_ref, w_ref, wdt_ref, p_ref, dt_ref, n1_ref):
    j = pl.program_id(1)

    @pl.when(j == 0)
    def _():
        x = x_ref[...]
        ms = jnp.mean(x * x, axis=-1, keepdims=True)
        y = x * lax.rsqrt(ms + EPS) * g_ref[...]
        n1 = (y * (1.0 + sc_ref[...]) + sh_ref[...]).astype(BF16)
        n1_ref[...] = n1
        dt_ref[...] = _dot(n1, wdt_ref[...])

    p_ref[...] = _dot(n1_ref[...], w_ref[...]).astype(p_ref.dtype)


def _inproj(x2, g, sc, sh, w_main, w_dt, p_dtype):
    s, d = x2.shape
    tm, tn = 1024, 1024
    tm = min(tm, s)
    row = lambda i, j: (0, 0)
    return pl.pallas_call(
        _inproj_kernel,
        grid=(s // tm, P_COLS // tn),
        in_specs=[pl.BlockSpec((tm, d), lambda i, j: (i, 0)),
                  pl.BlockSpec((1, d), row), pl.BlockSpec((1, d), row), pl.BlockSpec((1, d), row),
                  pl.BlockSpec((d, tn), lambda i, j: (0, j)),
                  pl.BlockSpec((d, LANES), row)],
        out_specs=[pl.BlockSpec((tm, tn), lambda i, j: (i, j)),
                   pl.BlockSpec((tm, LANES), lambda i, j: (i, 0))],
        out_shape=[jax.ShapeDtypeStruct((s, P_COLS), p_dtype),
                   jax.ShapeDtypeStruct((s, LANES), F32)],
        scratch_shapes=[pltpu.VMEM((tm, d), BF16)],
        compiler_params=_cparams(("arbitrary", "arbitrary")),
        name="inproj",
    )(x2, g, sc, sh, w_main, w_dt)


CV_TB = 256
CV_HALO = 32
CV_RC = 64
CV_LC = 128
SUBLANES = 8


def _conv_kernel(a_ref, b_ref, ah_ref, bh_ref, w_ref, cb_ref, lg_ref, lb_ref, wo_ref, o_ref, u_ref, v_ref):
    i = pl.program_id(0)
    u_ref[CV_HALO:, :] = a_ref[...].astype(F32) * _sigmoid(b_ref[...].astype(F32))
    uh = ah_ref[...].astype(F32) * _sigmoid(bh_ref[...].astype(F32))
    u_ref[0:CV_HALO, :] = jnp.where(i > 0, uh, 0.0)
    off0 = CV_HALO - (CONV_K - 1)
    win = CV_RC + CV_HALO
    for r in range(CV_TB // CV_RC):
        for c in range(D_MODEL // CV_LC):
            lanes = slice(c * CV_LC, (c + 1) * CV_LC)
            window = u_ref[r * CV_RC:r * CV_RC + win, lanes]
            acc = jnp.zeros((CV_RC, CV_LC), F32)
            for b in range(SUBLANES):
                shifted = window if b == 0 else pltpu.roll(window, win - b, axis=0)
                for k in range(CONV_K):
                    if (off0 + k) % SUBLANES != b:
                        continue
                    a8 = off0 + k - b
                    acc = acc + w_ref[k:k + 1, lanes] * shifted[a8:a8 + CV_RC, :]
            v_ref[r * CV_RC:(r + 1) * CV_RC, lanes] = acc
    v = v_ref[...] + cb_ref[...]
    mu = jnp.mean(v, axis=-1, keepdims=True)
    vc = v - mu
    var = jnp.mean(vc * vc, axis=-1, keepdims=True)
    y = vc * lax.rsqrt(var + EPS) * lg_ref[...] + lb_ref[...]
    o_ref[...] = _dot(_silu(y).astype(BF16), wo_ref[...]).astype(o_ref.dtype)


def _conv_branch(p, cv_w, cv_b, ln_g, ln_b, w_out):
    s = p.shape[0]
    tb = CV_TB
    hb = tb // CV_HALO
    d = D_MODEL
    row = lambda i: (0, 0)
    halo = lambda col: (lambda i: (jnp.maximum(i * hb - 1, 0), col))
    return pl.pallas_call(
        _conv_kernel,
        grid=(s // tb,),
        in_specs=[pl.BlockSpec((tb, d), lambda i: (i, 2)),
                  pl.BlockSpec((tb, d), lambda i: (i, 6)),
                  pl.BlockSpec((CV_HALO, d), halo(2)),
                  pl.BlockSpec((CV_HALO, d), halo(6)),
                  pl.BlockSpec((32, d), row),
                  pl.BlockSpec((1, d), row), pl.BlockSpec((1, d), row), pl.BlockSpec((1, d), row),
                  pl.BlockSpec((d, d), row)],
        out_specs=pl.BlockSpec((tb, d), lambda i: (i, 0)),
        out_shape=jax.ShapeDtypeStruct((s, d), BF16),
        scratch_shapes=[pltpu.VMEM((CV_HALO + tb, d), F32), pltpu.VMEM((tb, d), F32)],
        compiler_params=_cparams(("arbitrary",)),
        name="conv",
    )(p, p, p, p, cv_w, cv_b, ln_g, ln_b, w_out)


SSD_TB = 256
SSD_NC = SSD_TB // CHUNK
SSD_HALO = 16
SSD_LC = 512


def _ssd_kernel(xbc_ref, xbch_ref, dt_ref, z_ref, cw_ref, cbias_ref, dtb_ref, alog_ref, dskip_ref,
                ng_ref, wo_ref, e_ref, tril_ref, eye_ref, trilt_ref, o_ref,
                xc_ref, xconv_ref, yn_ref, st_ref):
    i = pl.program_id(0)

    @pl.when(i == 0)
    def _():
        st_ref[...] = jnp.zeros_like(st_ref)

    xc_ref[SSD_HALO:, :] = xbc_ref[...].astype(F32)
    xc_ref[0:SSD_HALO, :] = jnp.where(i > 0, xbch_ref[...].astype(F32), 0.0)
    win = CHUNK + SUBLANES
    for r in range(SSD_NC):
        for c in range(D_XBC // SSD_LC):
            lanes = slice(c * SSD_LC, (c + 1) * SSD_LC)
            r0 = SSD_HALO + r * CHUNK - SUBLANES
            window = xc_ref[r0:r0 + win, lanes]
            acc = cbias_ref[:, lanes] + cw_ref[SSM_CONV_K - 1:SSM_CONV_K, lanes] * window[SUBLANES:, :]
            for sft in range(1, SSM_CONV_K):
                k = SSM_CONV_K - 1 - sft
                shifted = pltpu.roll(window, sft, axis=0)
                acc = acc + cw_ref[k:k + 1, lanes] * shifted[SUBLANES:, :]
            xconv_ref[r * CHUNK:(r + 1) * CHUNK, lanes] = _silu(acc)

    a_row = -jnp.exp(alog_ref[...])
    e_mat = e_ref[...]
    tril = tril_ref[...]
    rr = lax.broadcasted_iota(I32, (2 * CHUNK, LANES), 0)
    cc = lax.broadcasted_iota(I32, (2 * CHUNK, LANES), 1)
    pair_mask = (rr < CHUNK) == (cc < HEADDIM)

    def chunk_body(c, carry):
        r0 = pl.multiple_of(c * CHUNK, CHUNK)
        xs = xconv_ref[pl.ds(r0, CHUNK), 0:D_INNER]
        bm = xconv_ref[pl.ds(r0, CHUNK), D_INNER:D_INNER + GROUPS * STATE].astype(BF16)
        cm = xconv_ref[pl.ds(r0, CHUNK), D_INNER + GROUPS * STATE:D_XBC].astype(BF16)
        dtr = dt_ref[pl.ds(r0, CHUNK), :] + dtb_ref[...]
        dt = jnp.maximum(dtr, 0.0) + jnp.log(1.0 + jnp.exp(-jnp.abs(dtr)))
        ad = dt * a_row
        a3 = _split3(ad)
        acs_c = _dot(tril, a3[0]) + _dot(tril, a3[1]) + _dot(tril, a3[2])
        c3 = _split3(acs_c)
        d3 = _split3(dt)
        ex = _dot(jnp.concatenate(c3 + d3, axis=0), e_mat)
        acs = ex[0:CHUNK] + ex[CHUNK:2 * CHUNK] + ex[2 * CHUNK:3 * CHUNK]
        dtb = ex[3 * CHUNK:4 * CHUNK] + ex[4 * CHUNK:5 * CHUNK] + ex[5 * CHUNK:6 * CHUNK]
        acs_row = jnp.sum(acs * eye_ref[...], axis=0, keepdims=True)
        seg = acs - acs_row
        tot = acs[CHUNK - 1:CHUNK, :]
        e_acs = jnp.exp(acs)
        decay = jnp.exp(tot - acs)
        cdec = jnp.exp(tot)
        lmat = jnp.exp(jnp.where(trilt_ref[...] > 0.0, seg, -jnp.inf))
        xdt = xs * dtb
        xd = (xdt * decay).astype(BF16)
        xdt_b = xdt.astype(BF16)
        y_parts = []
        for g in range(GROUPS):
            cg = cm[:, g * STATE:(g + 1) * STATE]
            bg = bm[:, g * STATE:(g + 1) * STATE]
            b2 = jnp.concatenate([bg, bg], axis=0)
            cb2 = lax.dot_general(cg, b2, (((1,), (1,)), ((), ())), preferred_element_type=F32)
            gl = slice(g * HPG * HEADDIM, (g + 1) * HPG * HEADDIM)
            st_g = st_ref[g]
            y_off = _dot(cg, st_g.astype(BF16)) * e_acs[:, gl]
            yd = []
            for k in range(HPG // 2):
                pl_ = slice(g * HPG * HEADDIM + k * LANES, g * HPG * HEADDIM + (k + 1) * LANES)
                gp = (cb2 * lmat[:, pl_]).astype(BF16)
                xp = xdt_b[:, pl_]
                x2 = jnp.where(pair_mask, jnp.concatenate([xp, xp], axis=0), jnp.zeros((), BF16))
                yd.append(_dot(gp, x2))
            y_parts.append(jnp.concatenate(yd, axis=1) + y_off)
            upd = lax.dot_general(bg, xd[:, gl], (((0,), (0,)), ((), ())), preferred_element_type=F32)
            st_ref[g] = st_g * cdec[:, gl] + upd
        y = jnp.concatenate(y_parts, axis=1) + dskip_ref[...] * xs
        y = y * _silu(z_ref[pl.ds(r0, CHUNK), :].astype(F32))
        outs = []
        gw = HPG * HEADDIM
        for g in range(GROUPS):
            v = y[:, g * gw:(g + 1) * gw]
            ms = jnp.mean(v * v, axis=-1, keepdims=True)
            outs.append(v * lax.rsqrt(ms + EPS))
        yn = jnp.concatenate(outs, axis=1) * ng_ref[...]
        yn_ref[pl.ds(r0, CHUNK), :] = yn.astype(BF16)
        return carry

    lax.fori_loop(0, SSD_NC, chunk_body, 0)
    o_ref[...] = _dot(yn_ref[...], wo_ref[...]).astype(o_ref.dtype)


def _ssd_branch(p, dt_raw, conv_w, conv_b, dt_bias, a_log, dskip_row, norm_g, w_out, e_mat, tril, eye_t, trilt):
    s = p.shape[0]
    tb = SSD_TB
    hb = tb // SSD_HALO
    row = lambda i: (0, 0)
    return pl.pallas_call(
        _ssd_kernel,
        grid=(s // tb,),
        in_specs=[pl.BlockSpec((tb, D_XBC), lambda i: (i, 1)),
                  pl.BlockSpec((SSD_HALO, D_XBC), lambda i: (jnp.maximum(i * hb - 1, 0), 1)),
                  pl.BlockSpec((tb, LANES), lambda i: (i, 0)),
                  pl.BlockSpec((tb, D_INNER), lambda i: (i, 0)),
                  pl.BlockSpec((8, D_XBC), row),
                  pl.BlockSpec((1, D_XBC), row),
                  pl.BlockSpec((1, LANES), row),
                  pl.BlockSpec((1, LANES), row),
                  pl.BlockSpec((1, D_INNER), row),
                  pl.BlockSpec((1, D_INNER), row),
                  pl.BlockSpec((D_INNER, D_MODEL), row),
                  pl.BlockSpec((LANES, D_INNER), row),
                  pl.BlockSpec((CHUNK, CHUNK), row),
                  pl.BlockSpec((CHUNK, D_INNER), row),
                  pl.BlockSpec((CHUNK, D_INNER), row)],
        out_specs=pl.BlockSpec((tb, D_MODEL), lambda i: (i, 0)),
        out_shape=jax.ShapeDtypeStruct((s, D_MODEL), BF16),
        scratch_shapes=[pltpu.VMEM((SSD_HALO + tb, D_XBC), F32),
                        pltpu.VMEM((tb, D_XBC), F32),
                        pltpu.VMEM((tb, D_INNER), BF16),
                        pltpu.VMEM((GROUPS, STATE, HPG * HEADDIM), F32)],
        compiler_params=_cparams(("arbitrary",)),
        name="ssd",
    )(p, p, dt_raw, p, conv_w, conv_b, dt_bias, a_log, dskip_row, norm_g, w_out, e_mat, tril, eye_t, trilt)


MG_TM = 512


def _merge_kernel(ycv_ref, yssm_ref, gcv_ref, gssm_ref, x_ref, g1_ref, wmix_ref, n2g_ref, sc_ref, sh_ref,
                  wrh_ref, wrl_ref, br_ref, h_ref, n2_ref, idx_ref, wts_ref):
    tm = x_ref.shape[0]
    d_tiles = x_ref.shape[1] // LANES
    m = (_sigmoid(gcv_ref[...].astype(F32)) * ycv_ref[...].astype(F32)
         + _sigmoid(gssm_ref[...].astype(F32)) * yssm_ref[...].astype(F32))
    mo = _dot(m.astype(BF16), wmix_ref[...])
    h = x_ref[...] + g1_ref[...] * mo
    h_ref[...] = h
    ms = jnp.mean(h * h, axis=-1, keepdims=True)
    n2 = (h * lax.rsqrt(ms + EPS) * n2g_ref[...]) * (1.0 + sc_ref[...]) + sh_ref[...]
    for q in range(d_tiles):
        n2_ref[pl.ds(q, tm, stride=d_tiles), :] = n2[:, q * LANES:(q + 1) * LANES]
    hi = n2.astype(BF16)
    lo = (n2 - hi.astype(F32)).astype(BF16)
    logits = _dot(hi, wrh_ref[...]) + _dot(lo, wrh_ref[...]) + _dot(hi, wrl_ref[...]) + br_ref[...]
    lane = lax.broadcasted_iota(I32, logits.shape, 1)
    lane_f = lane.astype(F32)
    neg = jnp.float32(-jnp.inf)
    big = jnp.float32(1e9)
    is_grp = (lane >= N_EXP) & (lane < N_EXP + N_GRP)
    gl = jnp.where(is_grp, logits, neg)
    gmax = jnp.max(gl, axis=-1, keepdims=True)
    gsum = jnp.sum(jnp.where(is_grp, jnp.exp(gl - gmax), 0.0), axis=-1, keepdims=True)
    g_p = 1.0 / gsum
    g_lane = jnp.min(jnp.where(gl == gmax, lane_f, big), axis=-1, keepdims=True)
    g_idx = g_lane.astype(I32) - N_EXP
    in_grp = (lane >= g_idx * EPG) & (lane < g_idx * EPG + EPG)
    el = jnp.where(in_grp, logits, neg)
    m1 = jnp.max(el, axis=-1, keepdims=True)
    i1 = jnp.min(jnp.where(el == m1, lane_f, big), axis=-1, keepdims=True)
    el2 = jnp.where(lane_f == i1, neg, el)
    m2 = jnp.max(el2, axis=-1, keepdims=True)
    i2 = jnp.min(jnp.where(el2 == m2, lane_f, big), axis=-1, keepdims=True)
    r = jnp.exp(m2 - m1)
    w0 = g_p / (1.0 + r)
    w1 = g_p * r / (1.0 + r)
    idx_ref[...] = jnp.where(lane == 0, i1.astype(I32), jnp.where(lane == 1, i2.astype(I32), 0))
    wts_ref[...] = jnp.where(lane == 0, w0, jnp.where(lane == 1, w1, 0.0))


def _merge(ycv, yssm, p, x2, g1, w_mix, n2g, sc2, sh2, wr_hi, wr_lo, b_r):
    s, d = x2.shape
    tm = min(MG_TM, s)
    row = lambda i: (0, 0)
    blk = lambda i: (i, 0)
    return pl.pallas_call(
        _merge_kernel,
        grid=(s // tm,),
        in_specs=[pl.BlockSpec((tm, d), blk), pl.BlockSpec((tm, d), blk),
                  pl.BlockSpec((tm, d), lambda i: (i, 7)), pl.BlockSpec((tm, d), lambda i: (i, 8)),
                  pl.BlockSpec((tm, d), blk),
                  pl.BlockSpec((1, d), row),
                  pl.BlockSpec((d, d), row),
                  pl.BlockSpec((1, d), row), pl.BlockSpec((1, d), row), pl.BlockSpec((1, d), row),
                  pl.BlockSpec((d, LANES), row), pl.BlockSpec((d, LANES), row), pl.BlockSpec((1, LANES), row)],
        out_specs=[pl.BlockSpec((tm, d), blk), pl.BlockSpec((tm * (d // LANES), LANES), blk),
                   pl.BlockSpec((tm, LANES), blk), pl.BlockSpec((tm, LANES), blk)],
        out_shape=[jax.ShapeDtypeStruct((s, d), F32), jax.ShapeDtypeStruct((s * (d // LANES), LANES), F32),
                   jax.ShapeDtypeStruct((s, LANES), I32), jax.ShapeDtypeStruct((s, LANES), F32)],
        compiler_params=_cparams(("arbitrary",)),
        name="merge",
    )(ycv, yssm, p, p, x2, g1, w_mix, n2g, sc2, sh2, wr_hi, wr_lo, b_r)


PLAN_TB = 512


def _plan_kernel(idx_ref, stril_ref, utri_ref, dest_ref, blk_ref, cnt_ref, run_ref, pstart_ref):
    ph = pl.program_id(0)
    i = pl.program_id(1)
    idx = idx_ref[...]
    lane = lax.broadcasted_iota(I32, idx.shape, 1)
    e0 = idx[:, 0:1]
    e1 = idx[:, 1:2]
    oh0 = (lane == e0).astype(F32)
    oh1 = (lane == e1).astype(F32)
    ohs = oh0 + oh1

    @pl.when((ph == 0) & (i == 0))
    def _():
        cnt_ref[...] = jnp.zeros_like(cnt_ref)

    @pl.when(ph == 0)
    def _():
        cnt_ref[...] += jnp.sum(ohs, axis=0, keepdims=True)

    @pl.when((ph == 1) & (i == 0))
    def _():
        cnt = cnt_ref[...]
        units = jnp.floor((cnt + (ROW_BLK - 1)) * (1.0 / ROW_BLK))
        u8 = jnp.broadcast_to(units, (8, LANES)).astype(BF16)
        pend = _dot(u8, utri_ref[...])
        pstart_ref[...] = (pend[0:1] - units) * ROW_BLK
        run_ref[...] = jnp.zeros_like(run_ref)
        nb = blk_ref.shape[0]
        b = lax.broadcasted_iota(I32, (nb, LANES), 0).astype(F32)
        ln = lax.broadcasted_iota(I32, (nb, LANES), 1)
        le = ((pend[0:1] <= b) & (ln < N_EXP)).astype(F32)
        be = jnp.minimum(jnp.sum(le, axis=-1, keepdims=True), N_EXP - 1)
        total = jnp.max(pend[0:1], axis=-1, keepdims=True)
        own_end = jnp.sum(jnp.where(ln.astype(F32) == be, pend[0:1], 0.0), axis=-1, keepdims=True)
        nxt = jnp.sum(((pend[0:1] <= own_end) & (ln < N_EXP)).astype(F32), axis=-1, keepdims=True)
        blk_ref[...] = jnp.where(ln == 0, be, jnp.where(ln == 1, total, jnp.where(ln == 2, nxt, 0.0))).astype(I32)

    @pl.when(ph == 1)
    def _():
        prefix = _dot(stril_ref[...], ohs.astype(BF16))
        base = prefix + run_ref[...] + pstart_ref[...]
        d0 = jnp.sum(oh0 * base, axis=-1, keepdims=True)
        d1 = jnp.sum(oh1 * base, axis=-1, keepdims=True)
        dest_ref[...] = jnp.where(lane == 0, d0, jnp.where(lane == 1, d1, 0.0)).astype(I32)
        run_ref[...] += jnp.sum(ohs, axis=0, keepdims=True)


def _plan(idx, stril, utri, n_blocks_pad):
    s = idx.shape[0]
    tb = min(PLAN_TB, s)
    row = lambda ph, i: (0, 0)
    return pl.pallas_call(
        _plan_kernel,
        grid=(2, s // tb),
        in_specs=[pl.BlockSpec((tb, LANES), lambda ph, i: (i, 0)),
                  pl.BlockSpec((tb, tb), row),
                  pl.BlockSpec((LANES, LANES), row)],
        out_specs=[pl.BlockSpec((tb, LANES), lambda ph, i: (i * ph, 0)),
                   pl.BlockSpec((n_blocks_pad, LANES), row)],
        out_shape=[jax.ShapeDtypeStruct((s, LANES), I32),
                   jax.ShapeDtypeStruct((n_blocks_pad, LANES), I32)],
        scratch_shapes=[pltpu.VMEM((1, LANES), F32), pltpu.VMEM((1, LANES), F32), pltpu.VMEM((1, LANES), F32)],
        compiler_params=_cparams(("arbitrary", "arbitrary")),
        name="plan",
    )(idx, stril, utri)


DSP_TB = 256
ROW_TILES = D_MODEL // LANES


def _dispatch_kernel(dest_ref, n2_ref, xs_in_ref, xs_ref, stage_ref, sem):
    del xs_in_ref
    i = pl.program_id(0)
    n = pl.num_programs(0)
    slot = i % 2
    rows = DSP_TB * ROW_TILES
    stage_ref[slot] = n2_ref[...]

    def start_body(j, c):
        src = stage_ref.at[slot, pl.ds(pl.multiple_of(j * ROW_TILES, ROW_TILES), ROW_TILES)]
        for k in range(2):
            d = pl.multiple_of(dest_ref[0, 0, 2 * j + k] * ROW_TILES, ROW_TILES)
            pltpu.make_async_copy(src, xs_ref.at[pl.ds(d, ROW_TILES)], sem.at[slot]).start()
        return c
    lax.fori_loop(0, DSP_TB, start_body, 0, unroll=4)

    def wait_all(sl):
        for _ in range(2):
            pltpu.make_async_copy(stage_ref.at[sl], xs_ref.at[pl.ds(0, rows)], sem.at[sl]).wait()

    @pl.when(i > 0)
    def _():
        wait_all(1 - slot)

    @pl.when(i == n - 1)
    def _():
        wait_all(slot)


def _dispatch(dest_flat, n2, xs_init):
    s = n2.shape[0] // ROW_TILES
    tb = min(DSP_TB, s)
    assert tb == DSP_TB
    return pl.pallas_call(
        _dispatch_kernel,
        grid=(s // tb,),
        in_specs=[pl.BlockSpec((1, 1, 2 * tb), lambda i: (i, 0, 0), memory_space=pltpu.SMEM),
                  pl.BlockSpec((tb * ROW_TILES, LANES), lambda i: (i, 0)),
                  pl.BlockSpec(memory_space=pl.ANY)],
        out_specs=pl.BlockSpec(memory_space=pl.ANY),
        out_shape=jax.ShapeDtypeStruct(xs_init.shape, xs_init.dtype),
        scratch_shapes=[pltpu.VMEM((2, tb * ROW_TILES, LANES), F32), pltpu.SemaphoreType.DMA((2,))],
        input_output_aliases={2: 0},
        compiler_params=_cparams(("arbitrary",)),
        name="dispatch",
    )(dest_flat, n2, xs_init)


def _experts_kernel(be_ref, nx_ref, nu_ref, xs_ref, w1_hbm, w3_hbm, w2_hbm, y_ref,
                    w1f_ref, w3f_ref, w2f_ref, w1b_ref, w3b_ref, w2b_ref, slot_ref, sem):
    b = pl.program_id(0)
    used = b < nu_ref[0]
    e = be_ref[b]
    prev = be_ref[jnp.maximum(b - 1, 0)]
    fresh = (b == 0) | (e != prev)

    def weight_copies(expert, sl):
        return (pltpu.make_async_copy(w1_hbm.at[expert], w1f_ref.at[sl], sem.at[sl]),
                pltpu.make_async_copy(w3_hbm.at[expert], w3f_ref.at[sl], sem.at[sl]),
                pltpu.make_async_copy(w2_hbm.at[expert], w2f_ref.at[sl], sem.at[sl]))

    @pl.when(b == 0)
    def _():
        slot_ref[0] = 0
        for cp in weight_copies(e, 0):
            cp.start()

    @pl.when(used & fresh)
    def _():
        sl = slot_ref[0]
        for cp in weight_copies(e, sl):
            cp.wait()
        w1b_ref[...] = w1f_ref[sl].astype(BF16)
        w3b_ref[...] = w3f_ref[sl].astype(BF16)
        w2b_ref[...] = w2f_ref[sl].astype(BF16)
        nxt = nx_ref[b]

        @pl.when(nxt < N_EXP)
        def _():
            for cp in weight_copies(nxt, 1 - sl):
                cp.start()
        slot_ref[0] = 1 - sl

    @pl.when(used)
    def _():
        x = jnp.concatenate([xs_ref[pl.ds(q, ROW_BLK, stride=ROW_TILES), :] for q in range(ROW_TILES)],
                            axis=1).astype(BF16)
        h1 = _dot(x, w1b_ref[...])
        h3 = _dot(x, w3b_ref[...])
        hdn = (_silu(h1) * h3).astype(BF16)
        y = _dot(hdn, w2b_ref[...])
        for q in range(ROW_TILES):
            y_ref[pl.ds(q, ROW_BLK, stride=ROW_TILES), :] = y[:, q * LANES:(q + 1) * LANES]

    @pl.when(jnp.logical_not(used))
    def _():
        y_ref[...] = jnp.zeros_like(y_ref)


def _experts(blk_e, blk_next, n_used, xs, w1, w3, w2):
    d = w1.shape[1]
    rb = ROW_BLK * ROW_TILES
    nb = xs.shape[0] // rb

    def xmap(b, be, nx, nu):
        return (jnp.minimum(b, jnp.maximum(nu[0] - 1, 0)), 0)

    grid_spec = pltpu.PrefetchScalarGridSpec(
        num_scalar_prefetch=3,
        grid=(nb,),
        in_specs=[pl.BlockSpec((rb, LANES), xmap),
                  pl.BlockSpec(memory_space=pl.ANY),
                  pl.BlockSpec(memory_space=pl.ANY),
                  pl.BlockSpec(memory_space=pl.ANY)],
        out_specs=pl.BlockSpec((rb, LANES), lambda b, be, nx, nu: (b, 0)),
        scratch_shapes=[pltpu.VMEM((2, d, D_FF), F32), pltpu.VMEM((2, d, D_FF), F32), pltpu.VMEM((2, D_FF, d), F32),
                        pltpu.VMEM((d, D_FF), BF16), pltpu.VMEM((d, D_FF), BF16), pltpu.VMEM((D_FF, d), BF16),
                        pltpu.SMEM((1,), I32), pltpu.SemaphoreType.DMA((2,))],
    )
    return pl.pallas_call(
        _experts_kernel,
        grid_spec=grid_spec,
        out_shape=jax.ShapeDtypeStruct(xs.shape, F32),
        compiler_params=_cparams(("arbitrary",)),
        name="experts",
    )(blk_e, blk_next, n_used, xs, w1, w3, w2)


CMB_TB = 256


def _combine_kernel(dcur_ref, dnxt_ref, yb_ref, h_ref, wts_ref, g2_ref, fg_ref, o_ref, buf_ref, sem):
    i = pl.program_id(0)
    n = pl.num_programs(0)
    na = 2 * CMB_TB
    rows = CMB_TB * ROW_TILES

    def issue(dref, sl):
        def body(j, c):
            jr = pl.multiple_of(j * ROW_TILES, ROW_TILES)
            for k in range(2):
                d = pl.multiple_of(dref[0, 0, 2 * j + k] * ROW_TILES, ROW_TILES)
                pltpu.make_async_copy(yb_ref.at[pl.ds(d, ROW_TILES)], buf_ref.at[sl, k, pl.ds(jr, ROW_TILES)],
                                      sem.at[sl]).start()
            return c
        lax.fori_loop(0, CMB_TB, body, 0, unroll=4)

    slot = i % 2

    @pl.when(i == 0)
    def _():
        issue(dcur_ref, 0)

    @pl.when(i + 1 < n)
    def _():
        issue(dnxt_ref, 1 - slot)

    for k in range(2):
        pltpu.make_async_copy(yb_ref.at[pl.ds(0, rows)], buf_ref.at[slot, k], sem.at[slot]).wait()

    w = wts_ref[...]
    w0 = w[:, 0:1]
    w1 = w[:, 1:2]
    hs = []
    ssq = jnp.zeros((CMB_TB, 1), F32)
    for q in range(ROW_TILES):
        lanes = slice(q * LANES, (q + 1) * LANES)
        moe = (w0 * buf_ref[slot, 0, pl.ds(q, CMB_TB, stride=ROW_TILES), :]
               + w1 * buf_ref[slot, 1, pl.ds(q, CMB_TB, stride=ROW_TILES), :])
        hq = h_ref[:, lanes] + g2_ref[:, lanes] * moe
        hs.append(hq)
        ssq = ssq + jnp.sum(hq * hq, axis=-1, keepdims=True)
    scale = lax.rsqrt(ssq * (1.0 / D_MODEL) + EPS)
    for q in range(ROW_TILES):
        lanes = slice(q * LANES, (q + 1) * LANES)
        o_ref[:, lanes] = hs[q] * scale * fg_ref[:, lanes]


def _combine(dest_flat, yb, h, wts, g2, fg):
    s, d = h.shape
    tb = min(CMB_TB, s)
    assert tb == CMB_TB
    nsteps = s // tb
    row = lambda i: (0, 0)
    blk = lambda i: (i, 0)
    return pl.pallas_call(
        _combine_kernel,
        grid=(nsteps,),
        in_specs=[pl.BlockSpec((1, 1, 2 * tb), lambda i: (i, 0, 0), memory_space=pltpu.SMEM),
                  pl.BlockSpec((1, 1, 2 * tb), lambda i: (jnp.minimum(i + 1, nsteps - 1), 0, 0), memory_space=pltpu.SMEM),
                  pl.BlockSpec(memory_space=pl.ANY),
                  pl.BlockSpec((tb, d), blk),
                  pl.BlockSpec((tb, LANES), blk),
                  pl.BlockSpec((1, d), row), pl.BlockSpec((1, d), row)],
        out_specs=pl.BlockSpec((tb, d), blk),
        out_shape=jax.ShapeDtypeStruct((s, d), F32),
        scratch_shapes=[pltpu.VMEM((2, 2, tb * ROW_TILES, LANES), F32), pltpu.SemaphoreType.DMA((2,))],
        compiler_params=_cparams(("arbitrary",)),
        name="combine",
    )(dest_flat, dest_flat, yb, h, wts, g2, fg)


def _pad_rows(a, rows):
    return jnp.pad(a, ((0, rows - a.shape[0]), (0, 0)))


def _pad_lanes(a, lanes):
    return jnp.pad(a, ((0, 0), (0, lanes - a.shape[1])))


def kernel(x, c, w_ada, b_ada, norm1_g, w_in, cv_dw_w, cv_dw_b, cv_ln_g, cv_ln_b, w_cv_out, ssm_conv_w,
           ssm_conv_b, dt_bias, a_log, d_skip, ssm_norm_g, w_ssm_out, w_mix_out, norm2_g, w_grp, b_grp, w_er,
           b_er, w1, w3, w2, final_g):
    bsz, seq, d = x.shape
    assert bsz == 1 and d == D_MODEL and w_ada.shape[0] == 1
    x2 = x.reshape(seq, d)

    mod = _ada(c.reshape(d, 1), w_ada[0], b_ada[0].reshape(1, -1))
    sh1, sc1, g1, sh2, sc2, g2 = [mod[:, k * d:(k + 1) * d] for k in range(6)]

    wi = w_in[0]
    o_glu, o_z, o_xbc, o_dt, o_gate = 0, 2 * d, 2 * d + D_INNER, 2 * d + D_INNER + D_XBC, 2 * d + D_INNER + D_XBC + HEADS
    w_main = jnp.concatenate([wi[:, o_z:o_z + D_INNER], wi[:, o_glu:o_glu + d], wi[:, o_xbc:o_xbc + D_XBC],
                              wi[:, o_glu + d:o_glu + 2 * d], wi[:, o_gate:o_gate + 2 * d]], axis=1).astype(BF16)
    w_dt = _pad_lanes(wi[:, o_dt:o_dt + HEADS], LANES).astype(BF16)
    row = lambda v: v.reshape(1, -1)

    p, dt_raw = _inproj(x2, row(norm1_g[0]), sc1, sh1, w_main, w_dt, BF16)

    ycv = _conv_branch(p, _pad_rows(cv_dw_w[0], 32), row(cv_dw_b[0]), row(cv_ln_g[0]), row(cv_ln_b[0]),
                       w_cv_out[0].astype(BF16))

    hh = jnp.arange(LANES)[:, None]
    jj = jnp.arange(D_INNER)[None, :]
    e_mat = ((jj // HEADDIM) == hh).astype(BF16)
    li = jnp.arange(CHUNK)[:, None]
    tril = (jnp.arange(CHUNK)[None, :] <= li).astype(BF16)
    sj = (jj % HEADDIM)
    eye_t = (li == sj).astype(F32)
    trilt = (sj <= li).astype(F32)
    yssm = _ssd_branch(p, dt_raw, _pad_rows(ssm_conv_w[0], 8), row(ssm_conv_b[0]),
                       _pad_lanes(row(dt_bias[0]), LANES), _pad_lanes(row(a_log[0]), LANES),
                       row(jnp.repeat(d_skip[0], HEADDIM)), row(ssm_norm_g[0]), w_ssm_out[0].astype(BF16),
                       e_mat, tril, eye_t, trilt)

    w_r = _pad_lanes(jnp.concatenate([w_er[0], w_grp[0]], axis=1), LANES)
    wr_hi = w_r.astype(BF16)
    wr_lo = (w_r - wr_hi.astype(F32)).astype(BF16)
    b_r = _pad_lanes(row(jnp.concatenate([b_er[0], b_grp[0]])), LANES)
    h, n2, idx, wts = _merge(ycv, yssm, p, x2, g1, w_mix_out[0].astype(BF16), row(norm2_g[0]), sc2, sh2,
                             wr_hi, wr_lo, b_r)

    n_asg = 2 * seq
    n_blocks = (n_asg + N_EXP * (ROW_BLK - 1)) // ROW_BLK
    nb_pad = -(-n_blocks // 8) * 8
    tp = min(PLAN_TB, seq)
    stril = (jnp.arange(tp)[None, :] < jnp.arange(tp)[:, None]).astype(BF16)
    utri = (jnp.arange(LANES)[:, None] <= jnp.arange(LANES)[None, :]).astype(BF16)
    dest, blk = _plan(idx, stril, utri, nb_pad)
    dest_flat = dest[:, 0:2].reshape(seq // DSP_TB, 1, 2 * DSP_TB)
    blk_e = blk[:n_blocks, 0]
    n_used = blk[0:1, 1]

    xs = _dispatch(dest_flat, n2, jnp.zeros((n_blocks * ROW_BLK * ROW_TILES, LANES), F32))
    yb = _experts(blk_e, blk[:n_blocks, 2], n_used, xs, w1[0], w3[0], w2[0])
    out = _combine(dest_flat, yb, h, wts, g2, row(final_g))
    return out.reshape(bsz, seq, d)
```
